```python
import math
import jax, jax.numpy as jnp
from jax import lax
import numpy as np

D_MODEL = 2048
BATCH = 8
SEQ = 2048
DEPTH = 2

CHUNK = 64

MIX_WIDTH = D_MODEL
CONV_CH = MIX_WIDTH // 2
SSM_CH = MIX_WIDTH - CONV_CH
CONV_WIDTH = 31
SSM_GROUP_WIDTH = 16
SSM_GROUPS = SSM_CH // SSM_GROUP_WIDTH
SSM_STATE = 64
IN_COLS = 2 * CONV_CH + 2 * SSM_CH

N_GROUPS = 4
EXPERTS_PER_GROUP = 8
TOP_K = 2
D_EXPERT = 512

EPS = 1e-6

kernel_name = 'hymba_conformer_s5_hmoe_trunk'


def rms_norm(x, g):
    xf = x.astype(jnp.float32)
    y = xf * lax.rsqrt(jnp.mean(xf * xf, axis=-1, keepdims=True) + EPS)
    return (y * g.astype(jnp.float32)).astype(x.dtype)


def layer_norm(x, g, b):
    xf = x.astype(jnp.float32)
    mu = jnp.mean(xf, axis=-1, keepdims=True)
    var = jnp.mean(jnp.square(xf - mu), axis=-1, keepdims=True)
    y = (xf - mu) * lax.rsqrt(var + EPS)
    return (y * g.astype(jnp.float32) + b.astype(jnp.float32)).astype(x.dtype)


def modulate(h, shift, scale):
    return h * (1.0 + scale[:, None, :]) + shift[:, None, :]


def conformer_conv(val, gate, dw_w, dw_b, ln_g, ln_b):
    z = val * jax.nn.sigmoid(gate)
    z = lax.conv_general_dilated(
        z, dw_w[:, None, :].astype(z.dtype), window_strides=(1,),
        padding=[(CONV_WIDTH - 1, 0)],
        dimension_numbers=('NWC', 'WIO', 'NWC'),
        feature_group_count=CONV_CH) + dw_b
    z = layer_norm(z, ln_g, ln_b)
    return jax.nn.silu(z)


def s5_ssm(u, gate, a_re, a_im, b_re, b_im, c_re, c_im, d_skip, log_dt):
    dtype = u.dtype
    bsz, seq, _ = u.shape
    f32 = jnp.float32
    uf = u.astype(f32).reshape(bsz, seq, SSM_GROUPS, SSM_GROUP_WIDTH)
    dt = jnp.exp(log_dt.astype(f32))[:, None]
    lr, li = a_re.astype(f32), a_im.astype(f32)
    mag = jnp.exp(lr * dt)
    ang = li * dt
    abar_re, abar_im = mag * jnp.cos(ang), mag * jnp.sin(ang)
    den = lr * lr + li * li
    nr, ni = abar_re - 1.0, abar_im
    f_re = (nr * lr + ni * li) / den
    f_im = (ni * lr - nr * li) / den
    br, bi = b_re.astype(f32), b_im.astype(f32)
    bb_re = f_re[..., None] * br - f_im[..., None] * bi
    bb_im = f_re[..., None] * bi + f_im[..., None] * br
    bu_re = jnp.einsum('blgh,gph->blgp', uf, bb_re)
    bu_im = jnp.einsum('blgh,gph->blgp', uf, bb_im)
    a_seq_re = jnp.broadcast_to(abar_re, (1, seq, SSM_GROUPS, SSM_STATE))
    a_seq_im = jnp.broadcast_to(abar_im, (1, seq, SSM_GROUPS, SSM_STATE))

    def combine(e1, e2):
        a1r, a1i, b1r, b1i = e1
        a2r, a2i, b2r, b2i = e2
        ar = a1r * a2r - a1i * a2i
        ai = a1r * a2i + a1i * a2r
        brr = a2r * b1r - a2i * b1i + b2r
        bii = a2r * b1i + a2i * b1r + b2i
        return (ar, ai, brr, bii)

    _, _, s_re, s_im = lax.associative_scan(combine, (a_seq_re, a_seq_im, bu_re, bu_im), axis=1)
    y = (jnp.einsum('blgp,ghp->blgh', s_re, c_re.astype(f32))
         - jnp.einsum('blgp,ghp->blgh', s_im, c_im.astype(f32)))
    y = y + uf * d_skip.astype(f32).reshape(SSM_GROUPS, SSM_GROUP_WIDTH)
    y = y.reshape(bsz, seq, SSM_CH)
    return (jax.nn.gelu(y) * jax.nn.sigmoid(gate.astype(f32))).astype(dtype)


def hier_moe(h, rg_w, rg_b, re_w, re_b, w_gate, w_up, w_down):
    bsz, seq, d = h.shape
    f32 = jnp.float32
    t = h.reshape(bsz * seq, d)
    g_logits = (t @ rg_w + rg_b).astype(f32)
    g_prob = jax.nn.softmax(g_logits, axis=-1)
    g_idx = jnp.argmax(g_logits, axis=-1)
    g_w = jnp.take_along_axis(g_prob, g_idx[:, None], axis=-1)
    e_logits = (jnp.einsum('td,gde->tge', t, re_w) + re_b).astype(f32)
    e_sel = jnp.take_along_axis(e_logits, g_idx[:, None, None], axis=1)[:, 0]
    top_v, top_i = lax.top_k(e_sel, TOP_K)
    top_w = jax.nn.softmax(top_v, axis=-1) * g_w
    e_w = jnp.sum(jax.nn.one_hot(top_i, EXPERTS_PER_GROUP, dtype=f32) * top_w[..., None], axis=1)
    comb = (jax.nn.one_hot(g_idx, N_GROUPS, dtype=f32)[:, :, None] * e_w[:, None, :]).astype(t.dtype)
    y = jnp.zeros_like(t)
    for g in range(N_GROUPS):
        a = jnp.einsum('td,edf->tef', t, w_gate[g])
        u = jnp.einsum('td,edf->tef', t, w_up[g])
        hm = jax.nn.silu(a) * u * comb[:, g, :, None]
        y = y + jnp.einsum('tef,efd->td', hm, w_down[g])
    return y.reshape(bsz, seq, d)


def setup_inputs(seed: int = 0) -> dict:
    key = jax.random.key(seed)
    ks = jax.random.split(key, 32)
    f32 = jnp.float32
    L, D = DEPTH, D_MODEL
    nrm = lambda k, shape, s: jax.random.normal(k, shape, f32) * s
    gain = lambda k, shape: 1.0 + 0.02 * jax.random.normal(k, shape, f32)
    n_idx = jnp.arange(SSM_STATE, dtype=f32)
    a_re = -0.5 + 0.01 * jax.random.normal(ks[10], (L, SSM_GROUPS, SSM_STATE), f32)
    a_im = math.pi * n_idx + 0.01 * jax.random.normal(ks[11], (L, SSM_GROUPS, SSM_STATE), f32)
    log_dt = jax.random.uniform(ks[18], (L, SSM_GROUPS), f32, math.log(1e-3), math.log(1e-1))
    return {
        'x': jax.random.normal(ks[0], (BATCH, SEQ, D), f32),
        'c': jax.random.normal(ks[1], (BATCH, D), f32),
        'w_ada': nrm(ks[2], (L, D, 6 * D), 0.5 * D ** -0.5),
        'b_ada': nrm(ks[3], (L, 6 * D), 0.02),
        'norm1_g': gain(ks[4], (L, D)),
        'w_in': nrm(ks[5], (L, D, IN_COLS), D ** -0.5),
        'conv_w': nrm(ks[6], (L, CONV_WIDTH, CONV_CH), CONV_WIDTH ** -0.5),
        'conv_b': nrm(ks[7], (L, CONV_CH), 0.02),
        'conv_ln_g': gain(ks[8], (L, CONV_CH)),
        'conv_ln_b': nrm(ks[9], (L, CONV_CH), 0.02),
        'ssm_a_re': a_re,
        'ssm_a_im': a_im,
        'ssm_b_re': nrm(ks[12], (L, SSM_GROUPS, SSM_STATE, SSM_GROUP_WIDTH), (2 * SSM_GROUP_WIDTH) ** -0.5),
        'ssm_b_im': nrm(ks[13], (L, SSM_GROUPS, SSM_STATE, SSM_GROUP_WIDTH), (2 * SSM_GROUP_WIDTH) ** -0.5),
        'ssm_c_re': nrm(ks[14], (L, SSM_GROUPS, SSM_GROUP_WIDTH, SSM_STATE), (2 * SSM_STATE) ** -0.5),
        'ssm_c_im': nrm(ks[15], (L, SSM_GROUPS, SSM_GROUP_WIDTH, SSM_STATE), (2 * SSM_STATE) ** -0.5),
        'ssm_d': nrm(ks[16], (L, SSM_CH), 1.0),
        'ssm_log_dt': log_dt,
        'out_norm_conv_g': gain(ks[17], (L, CONV_CH)),
        'out_norm_ssm_g': gain(ks[19], (L, SSM_CH)),
        'w_out': nrm(ks[20], (L, MIX_WIDTH, D), MIX_WIDTH ** -0.5),
        'norm2_g': gain(ks[21], (L, D)),
        'router_group_w': nrm(ks[22], (L, D, N_GROUPS), D ** -0.5),
        'router_group_b': nrm(ks[23], (L, N_GROUPS), 0.01),
        'router_expert_w': nrm(ks[24], (L, N_GROUPS, D, EXPERTS_PER_GROUP), D ** -0.5),
        'router_expert_b': nrm(ks[25], (L, N_GROUPS, EXPERTS_PER_GROUP), 0.01),
        'exp_w_gate': nrm(ks[26], (L, N_GROUPS, EXPERTS_PER_GROUP, D, D_EXPERT), D ** -0.5),
        'exp_w_up': nrm(ks[27], (L, N_GROUPS, EXPERTS_PER_GROUP, D, D_EXPERT), D ** -0.5),
        'exp_w_down': nrm(ks[28], (L, N_GROUPS, EXPERTS_PER_GROUP, D_EXPERT, D), D_EXPERT ** -0.5),
        'final_norm_g': gain(ks[29], (D,)),
    }


def reference(x, c, w_ada, b_ada, norm1_g, w_in, conv_w, conv_b, conv_ln_g, conv_ln_b,
              ssm_a_re, ssm_a_im, ssm_b_re, ssm_b_im, ssm_c_re, ssm_c_im, ssm_d, ssm_log_dt,
              out_norm_conv_g, out_norm_ssm_g, w_out, norm2_g,
              router_group_w, router_group_b, router_expert_w, router_expert_b,
              exp_w_gate, exp_w_up, exp_w_down, final_norm_g):
    c_act = jax.nn.silu(c)
    for l in range(DEPTH):
        mod = c_act @ w_ada[l] + b_ada[l]
        shift1, scale1, gate1, shift2, scale2, gate2 = jnp.split(mod, 6, axis=-1)

        h = modulate(rms_norm(x, norm1_g[l]), shift1, scale1)
        p = h @ w_in[l]
        o = 0
        conv_val = p[..., o:o + CONV_CH]; o += CONV_CH
        conv_gate = p[..., o:o + CONV_CH]; o += CONV_CH
        ssm_in = p[..., o:o + SSM_CH]; o += SSM_CH
        ssm_gate = p[..., o:o + SSM_CH]
        y_conv = conformer_conv(conv_val, conv_gate, conv_w[l], conv_b[l], conv_ln_g[l], conv_ln_b[l])
        y_ssm = s5_ssm(ssm_in, ssm_gate, ssm_a_re[l], ssm_a_im[l], ssm_b_re[l], ssm_b_im[l],
                       ssm_c_re[l], ssm_c_im[l], ssm_d[l], ssm_log_dt[l])
        y = jnp.concatenate([rms_norm(y_conv, out_norm_conv_g[l]),
                             rms_norm(y_ssm, out_norm_ssm_g[l])], axis=-1) @ w_out[l]
        x = x + gate1[:, None, :] * y

        h = modulate(rms_norm(x, norm2_g[l]), shift2, scale2)
        y = hier_moe(h, router_group_w[l], router_group_b[l], router_expert_w[l], router_expert_b[l],
                     exp_w_gate[l], exp_w_up[l], exp_w_down[l])
        x = x + gate2[:, None, :] * y
    return rms_norm(x, final_norm_g)
```

```python
import functools

import jax
import jax.numpy as jnp
from jax import lax
from jax.experimental import pallas as pl
from jax.experimental.pallas import tpu as pltpu

EPS = 1e-6
F32 = jnp.float32
BF16 = jnp.bfloat16
I32 = jnp.int32

SUBLANES = 8
LANES = 128
MXU_DIM = 256
VMEM_LIMIT = 56 * 1024 * 1024
ROUTE_LANES = LANES
ROW_CHUNK = 16


def _params(sem):
    return pltpu.CompilerParams(dimension_semantics=sem, vmem_limit_bytes=VMEM_LIMIT)


def _for_rows(n_rows, chunk, body):
    def step(i, carry):
        body(pl.multiple_of(i * chunk, chunk))
        return carry
    lax.fori_loop(0, n_rows // chunk, step, 0)


def _per_batch(v, rows):
    return jnp.tile(v, (rows // SUBLANES, 1))


def _ada_kernel(c_ref, w_ref, b_ref, o_ref):
    c = c_ref[...]
    ca = (c * jax.nn.sigmoid(c)).astype(BF16)
    o_ref[...] = jnp.dot(ca, w_ref[...].astype(BF16), preferred_element_type=F32) + b_ref[...]


def _ada(c, w_ada, b_ada):
    depth, d, n = w_ada.shape
    bsz = c.shape[0]
    tn = min(512, n)
    return pl.pallas_call(
        _ada_kernel,
        grid=(depth, n // tn),
        in_specs=[pl.BlockSpec((bsz, d), lambda l, j: (0, 0)),
                  pl.BlockSpec((None, d, tn), lambda l, j: (l, 0, j)),
                  pl.BlockSpec((None, 1, tn), lambda l, j: (l, 0, j))],
        out_specs=pl.BlockSpec((None, bsz, tn), lambda l, j: (l, 0, j)),
        out_shape=jax.ShapeDtypeStruct((depth, bsz, n), F32),
        compiler_params=_params(("arbitrary", "arbitrary")),
    )(c, w_ada, b_ada.reshape(depth, 1, n))


def _to_tb(x):
    bsz, seq, d = x.shape
    return jnp.swapaxes(x, 0, 1).reshape(seq * bsz, d)


def _final_kernel(x_ref, g_ref, o_ref):
    g = g_ref[...]

    def chunk(r0):
        x = x_ref[pl.ds(r0, ROW_CHUNK), :]
        ms = jnp.mean(x * x, axis=-1, keepdims=True)
        o_ref[pl.ds(r0, ROW_CHUNK), :] = x * lax.rsqrt(ms + EPS) * g
    _for_rows(x_ref.shape[0], ROW_CHUNK, chunk)


def _final(xt, g, bsz):
    rows, d = xt.shape
    tm = min(512, rows)
    y = pl.pallas_call(
        _final_kernel,
        grid=(rows // tm,),
        in_specs=[pl.BlockSpec((tm, d), lambda i: (i, 0)),
                  pl.BlockSpec((1, d), lambda i: (0, 0))],
        out_specs=pl.BlockSpec((tm, d), lambda i: (i, 0)),
        out_shape=jax.ShapeDtypeStruct((rows, d), F32),
        compiler_params=_params(("arbitrary",)),
    )(xt, g.reshape(1, d))
    return jnp.swapaxes(y.reshape(rows // bsz, bsz, d), 0, 1)


def _inproj_kernel(x_ref, g_ref, sc_ref, sh_ref, w_ref, o_ref, h_scr):
    tm = x_ref.shape[0]

    @pl.when(pl.program_id(1) == 0)
    def _():
        g = g_ref[...]
        sc = _per_batch(1.0 + sc_ref[...], ROW_CHUNK)
        sh = _per_batch(sh_ref[...], ROW_CHUNK)

        def chunk(r0):
            x = x_ref[pl.ds(r0, ROW_CHUNK), :]
            ms = jnp.mean(x * x, axis=-1, keepdims=True)
            h = x * lax.rsqrt(ms + EPS) * g
            h_scr[pl.ds(r0, ROW_CHUNK), :] = (h * sc + sh).astype(BF16)
        _for_rows(tm, ROW_CHUNK, chunk)

    o_ref[...] = jnp.dot(h_scr[...], w_ref[...], preferred_element_type=F32)


def _inproj(xt, g, scale, shift, w_bf16):
    rows, d = xt.shape
    n = w_bf16.shape[1]
    tm = min(512, rows)
    tn = min(1024, n)
    return pl.pallas_call(
        _inproj_kernel,
        grid=(rows // tm, n // tn),
        in_specs=[pl.BlockSpec((tm, d), lambda i, j: (i, 0)),
                  pl.BlockSpec((1, d), lambda i, j: (0, 0)),
                  pl.BlockSpec((SUBLANES, d), lambda i, j: (0, 0)),
                  pl.BlockSpec((SUBLANES, d), lambda i, j: (0, 0)),
                  pl.BlockSpec((d, tn), lambda i, j: (0, j))],
        out_specs=pl.BlockSpec((tm, tn), lambda i, j: (i, j)),
        out_shape=jax.ShapeDtypeStruct((rows, n), F32),
        scratch_shapes=[pltpu.VMEM((tm, d), BF16)],
        compiler_params=_params(("arbitrary", "arbitrary")),
    )(xt, g.reshape(1, d), scale, shift, w_bf16)


def _conv_kernel(v_ref, gt_ref, w_ref, cb_ref, lg_ref, lb_ref, og_ref, o_ref, z_scr, *, width):
    rows, ch = v_ref.shape
    halo = (width - 1) * SUBLANES

    @pl.when(pl.program_id(0) == 0)
    def _():
        z_scr[pl.ds(0, halo), :] = jnp.zeros((halo, ch), F32)

    def glu(r0):
        v = v_ref[pl.ds(r0, ROW_CHUNK), :]
        gt = gt_ref[pl.ds(r0, ROW_CHUNK), :]
        z_scr[pl.ds(halo + r0, ROW_CHUNK), :] = v * jax.nn.sigmoid(gt)
    _for_rows(rows, ROW_CHUNK, glu)

    cb = cb_ref[...]
    lg = lg_ref[...]
    lb = lb_ref[...]
    og = og_ref[...]

    def conv(r0):
        acc = jnp.broadcast_to(cb, (ROW_CHUNK, ch))
        for k in range(width):
            acc = acc + w_ref[pl.ds(k, 1), :] * z_scr[pl.ds(r0 + k * SUBLANES, ROW_CHUNK), :]
        mu = jnp.mean(acc, axis=-1, keepdims=True)
        cen = acc - mu
        var = jnp.mean(cen * cen, axis=-1, keepdims=True)
        y = cen * lax.rsqrt(var + EPS) * lg + lb
        s = y * jax.nn.sigmoid(y)
        ms = jnp.mean(s * s, axis=-1, keepdims=True)
        o_ref[pl.ds(r0, ROW_CHUNK), :] = (s * lax.rsqrt(ms + EPS) * og).astype(BF16)
    _for_rows(rows, ROW_CHUNK, conv)

    z_scr[pl.ds(0, halo), :] = z_scr[pl.ds(rows, halo), :]


def _conv(p, conv_w, conv_b, ln_g, ln_b, out_g):
    rows = p.shape[0]
    width, ch = conv_w.shape
    halo = (width - 1) * SUBLANES
    tr = min(512, rows)
    assert tr >= halo
    vec = lambda a: a.reshape(1, ch)
    row_spec = lambda col: pl.BlockSpec((tr, ch), lambda i, col=col: (i, col))
    one = pl.BlockSpec((1, ch), lambda i: (0, 0))
    return pl.pallas_call(
        functools.partial(_conv_kernel, width=width),
        grid=(rows // tr,),
        in_specs=[row_spec(0), row_spec(1),
                  pl.BlockSpec((width, ch), lambda i: (0, 0)), one, one, one, one],
        out_specs=pl.BlockSpec((tr, ch), lambda i: (i, 0)),
        out_shape=jax.ShapeDtypeStruct((rows, ch), BF16),
        scratch_shapes=[pltpu.VMEM((halo + tr, ch), F32)],
        compiler_params=_params(("arbitrary",)),
    )(p, p, conv_w, vec(conv_b), vec(ln_g), vec(ln_b), vec(out_g))


def _ssm_prep_kernel(lr_ref, li_ref, ldt_ref, br_ref, bi_ref, ar_o, ai_o, bbr_o, bbi_o):
    lr = lr_ref[...]
    li = li_ref[...]
    dt = jnp.exp(ldt_ref[...])
    mag = jnp.exp(lr * dt)
    ang = li * dt
    ar = mag * jnp.cos(ang)
    ai = mag * jnp.sin(ang)
    den = lr * lr + li * li
    nr = ar - 1.0
    ni = ai
    fr = (nr * lr + ni * li) / den
    fi = (ni * lr - nr * li) / den
    br = br_ref[...]
    bi = bi_ref[...]
    ar_o[...] = ar
    ai_o[...] = ai
    bbr_o[...] = fr * br - fi * bi
    bbi_o[...] = fr * bi + fi * br


def _ssm_prep(a_re, a_im, log_dt, b_re, b_im):
    depth, g, p = a_re.shape
    h = b_re.shape[-1]
    rep = lambda a: jnp.broadcast_to(a[:, :, None, :], (depth, g, h, p)).reshape(depth * g * h, p)
    ldt = jnp.broadcast_to(log_dt[:, :, None, None], (depth, g, h, p)).reshape(depth * g * h, p)
    bt = lambda b: jnp.swapaxes(b, -1, -2).reshape(depth * g * h, p)
    n = g * h
    spec = pl.BlockSpec((n, p), lambda l: (l, 0))
    shp = jax.ShapeDtypeStruct((depth * n, p), F32)
    ar, ai, bbr, bbi = pl.pallas_call(
        _ssm_prep_kernel,
        grid=(depth,),
        in_specs=[spec] * 5,
        out_specs=[spec] * 4,
        out_shape=[shp] * 4,
        compiler_params=_params(("arbitrary",)),
    )(rep(a_re), rep(a_im), ldt, bt(b_re), bt(b_im))
    r4 = lambda a: a.reshape(depth, g, h, p)
    return r4(ar)[:, :, 0, :], r4(ai)[:, :, 0, :], r4(bbr), r4(bbi)


def _ssm_block_weights(abar_re, abar_im, bb_re, bb_im, c_re, c_im):
    g, h, p = bb_re.shape
    gpt = min(MXU_DIM // h, g)
    kt = g // gpt
    eye = jnp.eye(gpt, dtype=F32)

    def bmat(bb):
        b4 = bb.reshape(kt, gpt, h, p)
        return jnp.einsum('kghp,gG->kghGp', b4, eye).reshape(kt, gpt * h, gpt * p)

    def cmat(c):
        c4 = c.reshape(kt, gpt, h, p)
        return jnp.einsum('kghp,gG->kgpGh', c4, eye).reshape(kt, gpt * p, gpt * h)

    bc = jnp.concatenate([bmat(bb_re), bmat(bb_im)], axis=-1).astype(BF16)
    cc = jnp.concatenate([cmat(c_re), -cmat(c_im)], axis=1).astype(BF16)
    bcast = lambda a: jnp.broadcast_to(a.reshape(1, g * p), (SUBLANES, g * p))
    return bc, cc, bcast(abar_re), bcast(abar_im)


def _ssm_kernel(u_ref, gt_ref, bc_ref, cc_ref, ar_ref, ai_ref, d_ref, og_ref, o_ref,
                s_scr, st_scr, y_scr, *, lane_chunk):
    rows, ch = u_ref.shape
    kt, kch, sw2 = bc_ref.shape
    sw = sw2 // 2
    steps = rows // SUBLANES

    @pl.when(pl.program_id(0) == 0)
    def _():
        st_scr[...] = jnp.zeros(st_scr.shape, F32)

    for k in range(kt):
        ub = u_ref[:, k * kch:(k + 1) * kch].astype(BF16)
        s_scr[:, k * sw2:(k + 1) * sw2] = jnp.dot(ub, bc_ref[k], preferred_element_type=F32)

    for k in range(kt):
        for c in range(sw // lane_chunk):
            re0 = k * sw2 + c * lane_chunk
            im0 = re0 + sw
            a0 = k * sw + c * lane_chunk
            ar = ar_ref[:, a0:a0 + lane_chunk]
            ai = ai_ref[:, a0:a0 + lane_chunk]

            def step(t, carry, re0=re0, im0=im0, ar=ar, ai=ai):
                sr, si = carry
                r = pl.multiple_of(t * SUBLANES, SUBLANES)
                br = s_scr[pl.ds(r, SUBLANES), re0:re0 + lane_chunk]
                bi = s_scr[pl.ds(r, SUBLANES), im0:im0 + lane_chunk]
                nr = ar * sr - ai * si + br
                ni = ar * si + ai * sr + bi
                s_scr[pl.ds(r, SUBLANES), re0:re0 + lane_chunk] = nr
                s_scr[pl.ds(r, SUBLANES), im0:im0 + lane_chunk] = ni
                return nr, ni

            sr, si = lax.fori_loop(
                0, steps, step,
                (st_scr[:, re0:re0 + lane_chunk], st_scr[:, im0:im0 + lane_chunk]))
            st_scr[:, re0:re0 + lane_chunk] = sr
            st_scr[:, im0:im0 + lane_chunk] = si

    for k in range(kt):
        sb = s_scr[:, k * sw2:(k + 1) * sw2].astype(BF16)
        y_scr[:, k * kch:(k + 1) * kch] = jnp.dot(sb, cc_ref[k], preferred_element_type=F32)

    d = d_ref[...]
    og = og_ref[...]

    def gate(r0):
        u = u_ref[pl.ds(r0, ROW_CHUNK), :]
        y = y_scr[pl.ds(r0, ROW_CHUNK), :] + u * d
        v = jax.nn.gelu(y) * jax.nn.sigmoid(gt_ref[pl.ds(r0, ROW_CHUNK), :])
        ms = jnp.mean(v * v, axis=-1, keepdims=True)
        o_ref[pl.ds(r0, ROW_CHUNK), :] = (v * lax.rsqrt(ms + EPS) * og).astype(BF16)
    _for_rows(rows, ROW_CHUNK, gate)


def _ssm(p, col0, bc, cc, ar, ai, d_skip, out_g):
    rows = p.shape[0]
    kt, kch, sw2 = bc.shape
    ch = kt * kch
    tr = min(256, rows)
    lane_chunk = min(512, sw2 // 2)
    vec = lambda a: a.reshape(1, ch)
    row_spec = lambda col: pl.BlockSpec((tr, ch), lambda i, col=col: (i, col))
    one = pl.BlockSpec((1, ch), lambda i: (0, 0))
    full = lambda a: pl.BlockSpec(a.shape, lambda i, nd=a.ndim: (0,) * nd)
    return pl.pallas_call(
        functools.partial(_ssm_kernel, lane_chunk=lane_chunk),
        grid=(rows // tr,),
        in_specs=[row_spec(col0), row_spec(col0 + 1), full(bc), full(cc), full(ar), full(ai), one, one],
        out_specs=pl.BlockSpec((tr, ch), lambda i: (i, 0)),
        out_shape=jax.ShapeDtypeStruct((rows, ch), BF16),
        scratch_shapes=[pltpu.VMEM((tr, kt * sw2), F32),
                        pltpu.VMEM((SUBLANES, kt * sw2), F32),
                        pltpu.VMEM((tr, ch), F32)],
        compiler_params=_params(("arbitrary",)),
    )(p, p, bc, cc, ar, ai, vec(d_skip), vec(out_g))


def _outproj_kernel(yc_ref, ys_ref, wo_ref, x_ref, g1_ref, n2_ref, sc_ref, sh_ref, wr_ref, br_ref,
                    x1_ref, h2_ref, ri_ref, rf_ref, cnt_ref, hb_scr, carry_scr,
                    *, n_groups, per_group):
    tm = x_ref.shape[0]
    i = pl.program_id(0)

    @pl.when(i == 0)
    def _():
        carry_scr[...] = jnp.zeros(carry_scr.shape, F32)

    x1_ref[...] = (jnp.dot(yc_ref[...], wo_ref[0], preferred_element_type=F32)
                   + jnp.dot(ys_ref[...], wo_ref[1], preferred_element_type=F32))

    g1 = _per_batch(g1_ref[...], ROW_CHUNK)
    sc = _per_batch(1.0 + sc_ref[...], ROW_CHUNK)
    sh = _per_batch(sh_ref[...], ROW_CHUNK)
    n2 = n2_ref[...]

    def chunk(r0):
        x1 = x_ref[pl.ds(r0, ROW_CHUNK), :] + g1 * x1_ref[pl.ds(r0, ROW_CHUNK), :]
        x1_ref[pl.ds(r0, ROW_CHUNK), :] = x1
        ms = jnp.mean(x1 * x1, axis=-1, keepdims=True)
        h2 = x1 * lax.rsqrt(ms + EPS) * n2 * sc + sh
        h2_ref[pl.ds(r0, ROW_CHUNK), :] = h2
        hb_scr[pl.ds(r0, ROW_CHUNK), :] = h2.astype(BF16)
    _for_rows(tm, ROW_CHUNK, chunk)

    lg = jnp.dot(hb_scr[...], wr_ref[...], preferred_element_type=F32) + br_ref[...]
    lane = lax.broadcasted_iota(I32, lg.shape, 1)
    neg = jnp.float32(-jnp.inf)
    big = jnp.int32(ROUTE_LANES)

    def first_max(vals):
        m = jnp.max(vals, axis=-1, keepdims=True)
        idx = jnp.min(jnp.where(vals == m, lane, big), axis=-1, keepdims=True)
        return m, idx

    gl = jnp.where(lane < n_groups, lg, neg)
    gmax, gidx = first_max(gl)
    g_w = 1.0 / jnp.sum(jnp.exp(gl - gmax), axis=-1, keepdims=True)
    lo = n_groups + gidx * per_group
    el = jnp.where((lane >= lo) & (lane < lo + per_group), lg, neg)
    v1, i1 = first_max(el)
    v2, i2 = first_max(jnp.where(lane == i1, neg, el))
    e21 = jnp.exp(v2 - v1)
    w1 = g_w / (1.0 + e21)
    w2 = g_w * e21 / (1.0 + e21)

    hit1 = lane == i1
    hit2 = lane == i2
    onehot = jnp.where(hit1 | hit2, 1.0, 0.0).astype(BF16)
    row = lax.broadcasted_iota(I32, (tm, tm), 0)
    col = lax.broadcasted_iota(I32, (tm, tm), 1)
    before = jnp.where(col < row, 1.0, 0.0).astype(BF16)
    cum = jnp.dot(before, onehot, preferred_element_type=F32) + carry_scr[...]
    rank1 = jnp.sum(jnp.where(hit1, cum, 0.0), axis=-1, keepdims=True)
    rank2 = jnp.sum(jnp.where(hit2, cum, 0.0), axis=-1, keepdims=True)
    carry_scr[...] = carry_scr[...] + jnp.sum(onehot.astype(F32), axis=0, keepdims=True)
    cnt_ref[...] = carry_scr[...]

    e1 = i1 - n_groups
    e2 = i2 - n_groups
    ri_ref[...] = jnp.where(lane == 0, e1, jnp.where(lane == 1, e2, jnp.where(
        lane == 2, rank1.astype(I32), jnp.where(lane == 3, rank2.astype(I32), 0))))
    rf_ref[...] = jnp.where(lane == 0, w1, jnp.where(lane == 1, w2, 0.0))


def _outproj(yc, ys, wo_bf16, xt, gate1, n2_g, scale2, shift2, wr_bf16, br, n_groups, per_group):
    rows, d = xt.shape
    ch = yc.shape[1]
    tm = min(256, rows)
    row_spec = lambda w: pl.BlockSpec((tm, w), lambda i: (i, 0))
    one = pl.BlockSpec((1, d), lambda i: (0, 0))
    per_b = pl.BlockSpec((SUBLANES, d), lambda i: (0, 0))
    kern = functools.partial(_outproj_kernel, n_groups=n_groups, per_group=per_group)
    return pl.pallas_call(
        kern,
        grid=(rows // tm,),
        in_specs=[row_spec(ch), row_spec(ch),
                  pl.BlockSpec((2, ch, d), lambda i: (0, 0, 0)),
                  row_spec(d), per_b, one, per_b, per_b,
                  pl.BlockSpec((d, ROUTE_LANES), lambda i: (0, 0)),
                  pl.BlockSpec((1, ROUTE_LANES), lambda i: (0, 0))],
        out_specs=[row_spec(d), row_spec(d), row_spec(ROUTE_LANES), row_spec(ROUTE_LANES),
                   pl.BlockSpec((1, ROUTE_LANES), lambda i: (0, 0))],
        out_shape=[jax.ShapeDtypeStruct((rows, d), F32),
                   jax.ShapeDtypeStruct((rows, d), F32),
                   jax.ShapeDtypeStruct((rows, ROUTE_LANES), I32),
                   jax.ShapeDtypeStruct((rows, ROUTE_LANES), F32),
                   jax.ShapeDtypeStruct((1, ROUTE_LANES), F32)],
        scratch_shapes=[pltpu.VMEM((tm, d), BF16), pltpu.VMEM((1, ROUTE_LANES), F32)],
        compiler_params=_params(("arbitrary",)),
    )(yc, ys, wo_bf16, xt, gate1, n2_g.reshape(1, d), scale2, shift2, wr_bf16, br)


def _dispatch_kernel(zrow_ref, zflag_ref, nv_ref, pos_ref, h_ref, hs_ref, zero_scr, sem, *, n_experts):
    tm = h_ref.shape[0]
    te = zero_scr.shape[0]
    n_tiles = hs_ref.shape[0] // te

    @pl.when(pl.program_id(0) == 0)
    def _():
        zero_scr[...] = jnp.zeros(zero_scr.shape, F32)

        def clear(row0):
            cp = pltpu.make_async_copy(zero_scr, hs_ref.at[pl.ds(pl.multiple_of(row0, te), te)], sem)
            cp.start()
            cp.wait()

        for e in range(n_experts):
            @pl.when(zflag_ref[e] > 0)
            def _(e=e):
                clear(zrow_ref[e])

        def clear_tail(t, carry):
            clear(t * te)
            return carry
        lax.fori_loop(nv_ref[0], n_tiles, clear_tail, 0)

    def row_copy(r, slot):
        return pltpu.make_async_copy(h_ref.at[pl.ds(r, 1)], hs_ref.at[pl.ds(slot, 1)], sem)

    def issue(r, carry):
        row_copy(r, pos_ref[0, 0, r]).start()
        row_copy(r, pos_ref[0, 1, r]).start()
        return carry
    lax.fori_loop(0, tm, issue, 0)

    def drain(r, carry):
        row_copy(r, pos_ref[0, 0, r]).wait()
        row_copy(r, pos_ref[0, 1, r]).wait()
        return carry
    lax.fori_loop(0, tm, drain, 0)


def _dispatch(h2, pos, zrow, zflag, n_valid, n_slots, tile_e):
    rows, d = h2.shape
    tm = pos.shape[-1]
    n_experts = zrow.shape[0]
    grid_spec = pltpu.PrefetchScalarGridSpec(
        num_scalar_prefetch=3,
        grid=(rows // tm,),
        in_specs=[pl.BlockSpec((1, 2, tm), lambda i, *_: (i, 0, 0), memory_space=pltpu.SMEM),
                  pl.BlockSpec((tm, d), lambda i, *_: (i, 0))],
        out_specs=pl.BlockSpec(memory_space=pl.ANY),
        scratch_shapes=[pltpu.VMEM((tile_e, d), F32), pltpu.SemaphoreType.DMA(())],
    )
    return pl.pallas_call(
        functools.partial(_dispatch_kernel, n_experts=n_experts),
        grid_spec=grid_spec,
        out_shape=jax.ShapeDtypeStruct((n_slots, d), F32),
        compiler_params=_params(("arbitrary",)),
    )(zrow, zflag, n_valid, pos, h2)


def _expert_kernel(te_ref, nv_ref, hs_ref, wg_ref, wu_ref, wd_ref, o_ref, wg_scr, wu_scr, wd_scr):
    i = pl.program_id(0)

    @pl.when(i < nv_ref[0])
    def _():
        @pl.when((i == 0) | (te_ref[i] != te_ref[jnp.maximum(i - 1, 0)]))
        def _():
            wg_scr[...] = wg_ref[...].astype(BF16)
            wu_scr[...] = wu_ref[...].astype(BF16)
            wd_scr[...] = wd_ref[...].astype(BF16)

        hb = hs_ref[...].astype(BF16)
        a = jnp.dot(hb, wg_scr[...], preferred_element_type=F32)
        u = jnp.dot(hb, wu_scr[...], preferred_element_type=F32)
        hm = (a * jax.nn.sigmoid(a) * u).astype(BF16)
        o_ref[...] = jnp.dot(hm, wd_scr[...], preferred_element_type=F32)

    @pl.when(i >= nv_ref[0])
    def _():
        o_ref[...] = jnp.zeros(o_ref.shape, F32)


def _experts(hs, w_gate, w_up, w_down, layer, tile_expert, n_valid, tile_e):
    n_slots, d = hs.shape
    per_group = w_gate.shape[2]
    f = w_gate.shape[-1]
    n_tiles = n_slots // tile_e

    def row_map(i, te, nv):
        return (jnp.minimum(i, nv[0] - 1), 0)

    def w_map(i, te, nv):
        e = te[i]
        return (layer, e // per_group, e % per_group, 0, 0)

    grid_spec = pltpu.PrefetchScalarGridSpec(
        num_scalar_prefetch=2,
        grid=(n_tiles,),
        in_specs=[pl.BlockSpec((tile_e, d), row_map),
                  pl.BlockSpec((None, None, None, d, f), w_map),
                  pl.BlockSpec((None, None, None, d, f), w_map),
                  pl.BlockSpec((None, None, None, f, d), w_map)],
        out_specs=pl.BlockSpec((tile_e, d), lambda i, te, nv: (i, 0)),
        scratch_shapes=[pltpu.VMEM((d, f), BF16), pltpu.VMEM((d, f), BF16), pltpu.VMEM((f, d), BF16)],
    )
    return pl.pallas_call(
        _expert_kernel,
        grid_spec=grid_spec,
        out_shape=jax.ShapeDtypeStruct((n_slots, d), F32),
        compiler_params=_params(("arbitrary",)),
    )(tile_expert, n_valid, hs, w_gate, w_up, w_down)


def _combine_kernel(pos_ref, eo_ref, rf_ref, x_ref, g2_ref, o_ref, a_scr, b_scr, sem):
    tm = x_ref.shape[0]

    def row_copy(slot, dst, r):
        return pltpu.make_async_copy(eo_ref.at[pl.ds(slot, 1)], dst.at[pl.ds(r, 1)], sem)

    def issue(r, carry):
        row_copy(pos_ref[0, 0, r], a_scr, r).start()
        row_copy(pos_ref[0, 1, r], b_scr, r).start()
        return carry
    lax.fori_loop(0, tm, issue, 0)

    def drain(r, carry):
        row_copy(pos_ref[0, 0, r], a_scr, r).wait()
        row_copy(pos_ref[0, 1, r], b_scr, r).wait()
        return carry
    lax.fori_loop(0, tm, drain, 0)

    g2 = _per_batch(g2_ref[...], ROW_CHUNK)

    def chunk(r0):
        rf = rf_ref[pl.ds(r0, ROW_CHUNK), :]
        y = rf[:, 0:1] * a_scr[pl.ds(r0, ROW_CHUNK), :] + rf[:, 1:2] * b_scr[pl.ds(r0, ROW_CHUNK), :]
        o_ref[pl.ds(r0, ROW_CHUNK), :] = x_ref[pl.ds(r0, ROW_CHUNK), :] + g2 * y
    _for_rows(tm, ROW_CHUNK, chunk)


def _combine(eo, pos, rf, x1, gate2):
    rows, d = x1.shape
    tm = pos.shape[-1]
    grid_spec = pl.GridSpec(
        grid=(rows // tm,),
        in_specs=[pl.BlockSpec((1, 2, tm), lambda i: (i, 0, 0), memory_space=pltpu.SMEM),
                  pl.BlockSpec(memory_space=pl.ANY),
                  pl.BlockSpec((tm, ROUTE_LANES), lambda i: (i, 0)),
                  pl.BlockSpec((tm, d), lambda i: (i, 0)),
                  pl.BlockSpec((SUBLANES, d), lambda i: (0, 0))],
        out_specs=pl.BlockSpec((tm, d), lambda i: (i, 0)),
        scratch_shapes=[pltpu.VMEM((tm, d), F32), pltpu.VMEM((tm, d), F32), pltpu.SemaphoreType.DMA(())],
    )
    return pl.pallas_call(
        _combine_kernel,
        grid_spec=grid_spec,
        out_shape=jax.ShapeDtypeStruct((rows, d), F32),
        compiler_params=_params(("arbitrary",)),
    )(pos, eo, rf, x1, gate2)


def _slots(ri, cnt, n_groups, n_experts, tile_e, tile_tok):
    rows = ri.shape[0]
    counts = cnt[0, n_groups:n_groups + n_experts].astype(I32)
    padded = ((counts + tile_e - 1) // tile_e) * tile_e
    ends = jnp.cumsum(padded)
    starts = ends - padded
    pos1 = starts[ri[:, 0]] + ri[:, 2]
    pos2 = starts[ri[:, 1]] + ri[:, 3]
    pos = jnp.stack([pos1.reshape(rows // tile_tok, tile_tok),
                     pos2.reshape(rows // tile_tok, tile_tok)], axis=1).astype(I32)
    n_tiles = (2 * rows + n_experts * (tile_e - 1)) // tile_e
    n_valid = ends[-1] // tile_e
    tile_start = jnp.arange(n_tiles, dtype=I32) * tile_e
    tile_expert = jnp.minimum(jnp.searchsorted(ends, tile_start, side='right'), n_experts - 1).astype(I32)
    last = tile_expert[jnp.maximum(n_valid - 1, 0)]
    tile_expert = jnp.where(jnp.arange(n_tiles) < n_valid, tile_expert, last).astype(I32)
    zrow = jnp.maximum(ends - tile_e, 0).astype(I32)
    zflag = (padded > 0).astype(I32)
    return pos, tile_expert, n_valid.reshape(1).astype(I32), zrow, zflag, n_tiles * tile_e


def kernel(x, c, w_ada, b_ada, norm1_g, w_in, conv_w, conv_b, conv_ln_g, conv_ln_b, ssm_a_re, ssm_a_im, ssm_b_re, ssm_b_im, ssm_c_re, ssm_c_im, ssm_d, ssm_log_dt, out_norm_conv_g, out_norm_ssm_g, w_out, norm2_g, router_group_w, router_group_b, router_expert_w, router_expert_b, exp_w_gate, exp_w_up, exp_w_down, final_norm_g):
    bsz, seq, d = x.shape
    assert bsz == SUBLANES
    depth = w_ada.shape[0]
    conv_ch = conv_w.shape[-1]
    ssm_ch = ssm_d.shape[-1]
    assert conv_ch == ssm_ch and w_in.shape[-1] == 2 * conv_ch + 2 * ssm_ch
    n_groups, per_group = router_expert_w.shape[1], router_expert_w.shape[3]
    n_experts = n_groups * per_group
    assert n_groups + n_experts <= ROUTE_LANES
    rows = bsz * seq
    tile_e = min(256, rows // 8)
    tile_tok = min(256, rows)

    mod = _ada(c, w_ada, b_ada)
    abar_re, abar_im, bb_re, bb_im = _ssm_prep(ssm_a_re, ssm_a_im, ssm_log_dt, ssm_b_re, ssm_b_im)
    xt = _to_tb(x)

    for l in range(depth):
        shift1, scale1, gate1, shift2, scale2, gate2 = [mod[l, :, k * d:(k + 1) * d] for k in range(6)]

        p = _inproj(xt, norm1_g[l], scale1, shift1, w_in[l].astype(BF16))
        yc = _conv(p, conv_w[l], conv_b[l], conv_ln_g[l], conv_ln_b[l], out_norm_conv_g[l])
        bc, cc, ar, ai = _ssm_block_weights(abar_re[l], abar_im[l], bb_re[l], bb_im[l], ssm_c_re[l], ssm_c_im[l])
        ys = _ssm(p, 2 * conv_ch // ssm_ch, bc, cc, ar, ai, ssm_d[l], out_norm_ssm_g[l])

        wr = jnp.concatenate([router_group_w[l]] + [router_expert_w[l, g] for g in range(n_groups)], axis=-1)
        wr = jnp.pad(wr, ((0, 0), (0, ROUTE_LANES - wr.shape[1]))).astype(BF16)
        br = jnp.concatenate([router_group_b[l], router_expert_b[l].reshape(-1)])
        br = jnp.pad(br, (0, ROUTE_LANES - br.shape[0])).reshape(1, ROUTE_LANES)
        wo = w_out[l].astype(BF16).reshape(2, conv_ch, d)
        x1, h2, ri, rf, cnt = _outproj(yc, ys, wo, xt, gate1, norm2_g[l], scale2, shift2, wr, br,
                                       n_groups, per_group)

        pos, tile_expert, n_valid, zrow, zflag, n_slots = _slots(ri, cnt, n_groups, n_experts, tile_e, tile_tok)
        hs = _dispatch(h2, pos, zrow, zflag, n_valid, n_slots, tile_e)
        eo = _experts(hs, exp_w_gate, exp_w_up, exp_w_down, l, tile_expert, n_valid, tile_e)
        xt = _combine(eo, pos, rf, x1, gate2)

    return _final(xt, final_norm_g, bsz)
```

```python
import functools

import jax
import jax.numpy as jnp
from jax import lax
from jax.experimental import pallas as pl
from jax.experimental.pallas import tpu as pltpu

EPS = 1e-6
F32 = jnp.float32
BF16 = jnp.bfloat16
I32 = jnp.int32

SUBLANES = 8
LANES = 128
MXU_DIM = 256
VMEM_LIMIT = 56 * 1024 * 1024
ROUTE_LANES = LANES
ROW_CHUNK = 16
CONV_ROWS = 32
CONV_LANES = 512


def _params(sem):
    return pltpu.CompilerParams(dimension_semantics=sem, vmem_limit_bytes=VMEM_LIMIT)


def _for_rows(n_rows, chunk, body, unroll=2):
    def step(i, carry):
        body(pl.multiple_of(i * chunk, chunk))
        return carry
    lax.fori_loop(0, n_rows // chunk, step, 0, unroll=unroll)


def _per_batch(v, rows):
    return jnp.tile(v, (rows // SUBLANES, 1))


def _ada_kernel(c_ref, w_ref, b_ref, o_ref):
    c = c_ref[...]
    ca = (c * jax.nn.sigmoid(c)).astype(BF16)
    o_ref[...] = jnp.dot(ca, w_ref[...].astype(BF16), preferred_element_type=F32) + b_ref[...]


def _ada(c, w_ada, b_ada):
    depth, d, n = w_ada.shape
    bsz = c.shape[0]
    tn = min(512, n)
    return pl.pallas_call(
        _ada_kernel,
        grid=(depth, n // tn),
        in_specs=[pl.BlockSpec((bsz, d), lambda l, j: (0, 0)),
                  pl.BlockSpec((None, d, tn), lambda l, j: (l, 0, j)),
                  pl.BlockSpec((None, 1, tn), lambda l, j: (l, 0, j))],
        out_specs=pl.BlockSpec((None, bsz, tn), lambda l, j: (l, 0, j)),
        out_shape=jax.ShapeDtypeStruct((depth, bsz, n), F32),
        compiler_params=_params(("arbitrary", "arbitrary")),
    )(c, w_ada, b_ada.reshape(depth, 1, n))


def _to_tb(x):
    bsz, seq, d = x.shape
    return jnp.swapaxes(x, 0, 1).reshape(seq * bsz, d)


def _final_kernel(x_ref, g_ref, o_ref):
    g = g_ref[...]

    def chunk(r0):
        x = x_ref[pl.ds(r0, ROW_CHUNK), :]
        ms = jnp.mean(x * x, axis=-1, keepdims=True)
        o_ref[pl.ds(r0, ROW_CHUNK), :] = x * lax.rsqrt(ms + EPS) * g
    _for_rows(x_ref.shape[0], ROW_CHUNK, chunk)


def _final(xt, g, bsz):
    rows, d = xt.shape
    tm = min(512, rows)
    y = pl.pallas_call(
        _final_kernel,
        grid=(rows // tm,),
        in_specs=[pl.BlockSpec((tm, d), lambda i: (i, 0)),
                  pl.BlockSpec((1, d), lambda i: (0, 0))],
        out_specs=pl.BlockSpec((tm, d), lambda i: (i, 0)),
        out_shape=jax.ShapeDtypeStruct((rows, d), F32),
        compiler_params=_params(("arbitrary",)),
    )(xt, g.reshape(1, d))
    return jnp.swapaxes(y.reshape(rows // bsz, bsz, d), 0, 1)


def _inproj_kernel(x_ref, g_ref, sc_ref, sh_ref, w_ref, o_ref, h_scr, *, tn):
    tm = x_ref.shape[0]
    g = g_ref[...]
    sc = _per_batch(1.0 + sc_ref[...], ROW_CHUNK)
    sh = _per_batch(sh_ref[...], ROW_CHUNK)

    def chunk(r0):
        x = x_ref[pl.ds(r0, ROW_CHUNK), :]
        ms = jnp.mean(x * x, axis=-1, keepdims=True)
        h = x * lax.rsqrt(ms + EPS) * g
        h_scr[pl.ds(r0, ROW_CHUNK), :] = (h * sc + sh).astype(BF16)
    _for_rows(tm, ROW_CHUNK, chunk)

    for n0 in range(0, w_ref.shape[1], tn):
        o_ref[:, n0:n0 + tn] = jnp.dot(
            h_scr[...], w_ref[:, n0:n0 + tn], preferred_element_type=F32).astype(o_ref.dtype)


def _inproj(xt, g, scale, shift, w_bf16):
    rows, d = xt.shape
    n = w_bf16.shape[1]
    tm = min(512, rows)
    tn = min(1024, n)
    return pl.pallas_call(
        functools.partial(_inproj_kernel, tn=tn),
        grid=(rows // tm,),
        in_specs=[pl.BlockSpec((tm, d), lambda i: (i, 0)),
                  pl.BlockSpec((1, d), lambda i: (0, 0)),
                  pl.BlockSpec((SUBLANES, d), lambda i: (0, 0)),
                  pl.BlockSpec((SUBLANES, d), lambda i: (0, 0)),
                  pl.BlockSpec((d, n), lambda i: (0, 0))],
        out_specs=pl.BlockSpec((tm, n), lambda i: (i, 0)),
        out_shape=jax.ShapeDtypeStruct((rows, n), BF16),
        scratch_shapes=[pltpu.VMEM((tm, d), BF16)],
        compiler_params=_params(("arbitrary",)),
    )(xt, g.reshape(1, d), scale, shift, w_bf16)


def _conv_kernel(v_ref, gt_ref, w_ref, cb_ref, lg_ref, lb_ref, og_ref, o_ref, z_scr, a_scr, *, width):
    rows, ch = v_ref.shape
    halo = (width - 1) * SUBLANES

    @pl.when(pl.program_id(0) == 0)
    def _():
        z_scr[pl.ds(0, halo), :] = jnp.zeros((halo, ch), F32)

    def glu(r0):
        v = v_ref[pl.ds(r0, ROW_CHUNK), :].astype(F32)
        gt = gt_ref[pl.ds(r0, ROW_CHUNK), :].astype(F32)
        z_scr[pl.ds(halo + r0, ROW_CHUNK), :] = v * jax.nn.sigmoid(gt)
    _for_rows(rows, ROW_CHUNK, glu)

    lg = lg_ref[...]
    lb = lb_ref[...]
    og = og_ref[...]
    reps = CONV_ROWS // SUBLANES
    lanes = min(CONV_LANES, ch)

    def taps(r0):
        for c0 in range(0, ch, lanes):
            cb = cb_ref[:, c0:c0 + lanes]
            acc = [cb] * reps
            w = {}
            for j in range(reps + width - 1):
                z = z_scr[pl.ds(r0 + j * SUBLANES, SUBLANES), c0:c0 + lanes]
                if j < width:
                    w[j] = w_ref[pl.ds(j * SUBLANES, SUBLANES), c0:c0 + lanes]
                for m in range(reps):
                    k = j - m
                    if 0 <= k < width:
                        acc[m] = acc[m] + w[k] * z
            for m in range(reps):
                a_scr[pl.ds(r0 + m * SUBLANES, SUBLANES), c0:c0 + lanes] = acc[m]
    _for_rows(rows, CONV_ROWS, taps, unroll=1)

    def conv(r0):
        acc = a_scr[pl.ds(r0, ROW_CHUNK), :]
        mu = jnp.mean(acc, axis=-1, keepdims=True)
        cen = acc - mu
        var = jnp.mean(cen * cen, axis=-1, keepdims=True)
        y = cen * lax.rsqrt(var + EPS) * lg + lb
        s = y * jax.nn.sigmoid(y)
        ms = jnp.mean(s * s, axis=-1, keepdims=True)
        o_ref[pl.ds(r0, ROW_CHUNK), :] = (s * lax.rsqrt(ms + EPS) * og).astype(BF16)
    _for_rows(rows, ROW_CHUNK, conv, unroll=4)

    z_scr[pl.ds(0, halo), :] = z_scr[pl.ds(rows, halo), :]


def _conv(p, conv_w, conv_b, ln_g, ln_b, out_g):
    rows = p.shape[0]
    width, ch = conv_w.shape
    halo = (width - 1) * SUBLANES
    tr = min(512, rows)
    assert tr >= halo
    vec = lambda a: a.reshape(1, ch)
    row_spec = lambda col: pl.BlockSpec((tr, ch), lambda i, col=col: (i, col))
    one = pl.BlockSpec((1, ch), lambda i: (0, 0))
    w8 = jnp.broadcast_to(conv_w[:, None, :], (width, SUBLANES, ch)).reshape(width * SUBLANES, ch)
    cb8 = jnp.broadcast_to(conv_b[None, :], (SUBLANES, ch))
    return pl.pallas_call(
        functools.partial(_conv_kernel, width=width),
        grid=(rows // tr,),
        in_specs=[row_spec(0), row_spec(1),
                  pl.BlockSpec((width * SUBLANES, ch), lambda i: (0, 0)),
                  pl.BlockSpec((SUBLANES, ch), lambda i: (0, 0)), one, one, one],
        out_specs=pl.BlockSpec((tr, ch), lambda i: (i, 0)),
        out_shape=jax.ShapeDtypeStruct((rows, ch), BF16),
        scratch_shapes=[pltpu.VMEM((halo + tr, ch), F32), pltpu.VMEM((tr, ch), F32)],
        compiler_params=_params(("arbitrary",)),
    )(p, p, w8, cb8, vec(ln_g), vec(ln_b), vec(out_g))


def _ssm_prep_kernel(lr_ref, li_ref, ldt_ref, br_ref, bi_ref, ar_o, ai_o, bbr_o, bbi_o):
    lr = lr_ref[...]
    li = li_ref[...]
    dt = jnp.exp(ldt_ref[...])
    mag = jnp.exp(lr * dt)
    ang = li * dt
    ar = mag * jnp.cos(ang)
    ai = mag * jnp.sin(ang)
    den = lr * lr + li * li
    nr = ar - 1.0
    ni = ai
    fr = (nr * lr + ni * li) / den
    fi = (ni * lr - nr * li) / den
    br = br_ref[...]
    bi = bi_ref[...]
    ar_o[...] = ar
    ai_o[...] = ai
    bbr_o[...] = fr * br - fi * bi
    bbi_o[...] = fr * bi + fi * br


def _ssm_prep(a_re, a_im, log_dt, b_re, b_im):
    depth, g, p = a_re.shape
    h = b_re.shape[-1]
    rep = lambda a: jnp.broadcast_to(a[:, :, None, :], (depth, g, h, p)).reshape(depth * g * h, p)
    ldt = jnp.broadcast_to(log_dt[:, :, None, None], (depth, g, h, p)).reshape(depth * g * h, p)
    bt = lambda b: jnp.swapaxes(b, -1, -2).reshape(depth * g * h, p)
    n = g * h
    spec = pl.BlockSpec((n, p), lambda l: (l, 0))
    shp = jax.ShapeDtypeStruct((depth * n, p), F32)
    ar, ai, bbr, bbi = pl.pallas_call(
        _ssm_prep_kernel,
        grid=(depth,),
        in_specs=[spec] * 5,
        out_specs=[spec] * 4,
        out_shape=[shp] * 4,
        compiler_params=_params(("arbitrary",)),
    )(rep(a_re), rep(a_im), ldt, bt(b_re), bt(b_im))
    r4 = lambda a: a.reshape(depth, g, h, p)
    return r4(ar)[:, :, 0, :], r4(ai)[:, :, 0, :], r4(bbr), r4(bbi)


def _ssm_block_weights(abar_re, abar_im, bb_re, bb_im, c_re, c_im):
    g, h, p = bb_re.shape
    gpt = min(MXU_DIM // h, g)
    kt = g // gpt
    eye = jnp.eye(gpt, dtype=F32)

    def bmat(bb):
        b4 = bb.reshape(kt, gpt, h, p)
        return jnp.einsum('kghp,gG->kghGp', b4, eye).reshape(kt, gpt * h, gpt * p)

    def cmat(c):
        c4 = c.reshape(kt, gpt, h, p)
        return jnp.einsum('kghp,gG->kgpGh', c4, eye).reshape(kt, gpt * p, gpt * h)

    bc = jnp.concatenate([bmat(bb_re), bmat(bb_im)], axis=-1).astype(BF16)
    cc = jnp.concatenate([cmat(c_re), -cmat(c_im)], axis=1).astype(BF16)
    bcast = lambda a: jnp.broadcast_to(a.reshape(1, g * p), (SUBLANES, g * p))
    return bc, cc, bcast(abar_re), bcast(abar_im)


def _ssm_kernel(u_ref, gt_ref, bc_ref, cc_ref, ar_ref, ai_ref, d_ref, og_ref, o_ref,
                s_scr, st_scr, y_scr, *, lane_chunk):
    rows, ch = u_ref.shape
    kt, kch, sw2 = bc_ref.shape
    sw = sw2 // 2
    steps = rows // SUBLANES

    @pl.when(pl.program_id(0) == 0)
    def _():
        st_scr[...] = jnp.zeros(st_scr.shape, F32)

    for k in range(kt):
        ub = u_ref[:, k * kch:(k + 1) * kch].astype(BF16)
        s_scr[:, k * sw2:(k + 1) * sw2] = jnp.dot(ub, bc_ref[k], preferred_element_type=F32)

    for k in range(kt):
        for c in range(sw // lane_chunk):
            re0 = k * sw2 + c * lane_chunk
            im0 = re0 + sw
            a0 = k * sw + c * lane_chunk
            ar = ar_ref[:, a0:a0 + lane_chunk]
            ai = ai_ref[:, a0:a0 + lane_chunk]

            def step(t, carry, re0=re0, im0=im0, ar=ar, ai=ai):
                sr, si = carry
                r = pl.multiple_of(t * SUBLANES, SUBLANES)
                br = s_scr[pl.ds(r, SUBLANES), re0:re0 + lane_chunk]
                bi = s_scr[pl.ds(r, SUBLANES), im0:im0 + lane_chunk]
                nr = ar * sr - ai * si + br
                ni = ar * si + ai * sr + bi
                s_scr[pl.ds(r, SUBLANES), re0:re0 + lane_chunk] = nr
                s_scr[pl.ds(r, SUBLANES), im0:im0 + lane_chunk] = ni
                return nr, ni

            sr, si = lax.fori_loop(
                0, steps, step,
                (st_scr[:, re0:re0 + lane_chunk], st_scr[:, im0:im0 + lane_chunk]))
            st_scr[:, re0:re0 + lane_chunk] = sr
            st_scr[:, im0:im0 + lane_chunk] = si

    for k in range(kt):
        sb = s_scr[:, k * sw2:(k + 1) * sw2].astype(BF16)
        y_scr[:, k * kch:(k + 1) * kch] = jnp.dot(sb, cc_ref[k], preferred_element_type=F32)

    d = d_ref[...]
    og = og_ref[...]

    def gate(r0):
        u = u_ref[pl.ds(r0, ROW_CHUNK), :].astype(F32)
        y = y_scr[pl.ds(r0, ROW_CHUNK), :] + u * d
        v = jax.nn.gelu(y) * jax.nn.sigmoid(gt_ref[pl.ds(r0, ROW_CHUNK), :].astype(F32))
        ms = jnp.mean(v * v, axis=-1, keepdims=True)
        o_ref[pl.ds(r0, ROW_CHUNK), :] = (v * lax.rsqrt(ms + EPS) * og).astype(BF16)
    _for_rows(rows, ROW_CHUNK, gate)


def _ssm(p, col0, bc, cc, ar, ai, d_skip, out_g):
    rows = p.shape[0]
    kt, kch, sw2 = bc.shape
    ch = kt * kch
    tr = min(256, rows)
    lane_chunk = min(512, sw2 // 2)
    vec = lambda a: a.reshape(1, ch)
    row_spec = lambda col: pl.BlockSpec((tr, ch), lambda i, col=col: (i, col))
    one = pl.BlockSpec((1, ch), lambda i: (0, 0))
    full = lambda a: pl.BlockSpec(a.shape, lambda i, nd=a.ndim: (0,) * nd)
    return pl.pallas_call(
        functools.partial(_ssm_kernel, lane_chunk=lane_chunk),
        grid=(rows // tr,),
        in_specs=[row_spec(col0), row_spec(col0 + 1), full(bc), full(cc), full(ar), full(ai), one, one],
        out_specs=pl.BlockSpec((tr, ch), lambda i: (i, 0)),
        out_shape=jax.ShapeDtypeStruct((rows, ch), BF16),
        scratch_shapes=[pltpu.VMEM((tr, kt * sw2), F32),
                        pltpu.VMEM((SUBLANES, kt * sw2), F32),
                        pltpu.VMEM((tr, ch), F32)],
        compiler_params=_params(("arbitrary",)),
    )(p, p, bc, cc, ar, ai, vec(d_skip), vec(out_g))


def _outproj_kernel(yc_ref, ys_ref, wo_ref, x_ref, g1_ref, n2_ref, sc_ref, sh_ref, wr_ref, br_ref,
                    x1_ref, h2_ref, lp_ref, lpt_ref, tab_ref, *, n_groups, per_group):
    tm = x_ref.shape[0]

    x1_ref[...] = (jnp.dot(yc_ref[...], wo_ref[0], preferred_element_type=F32)
                   + jnp.dot(ys_ref[...], wo_ref[1], preferred_element_type=F32))

    g1 = _per_batch(g1_ref[...], ROW_CHUNK)
    sc = _per_batch(1.0 + sc_ref[...], ROW_CHUNK)
    sh = _per_batch(sh_ref[...], ROW_CHUNK)
    n2 = n2_ref[...]

    def chunk(r0):
        x1 = x_ref[pl.ds(r0, ROW_CHUNK), :] + g1 * x1_ref[pl.ds(r0, ROW_CHUNK), :]
        x1_ref[pl.ds(r0, ROW_CHUNK), :] = x1
        ms = jnp.mean(x1 * x1, axis=-1, keepdims=True)
        h2 = x1 * lax.rsqrt(ms + EPS) * n2 * sc + sh
        h2_ref[pl.ds(r0, ROW_CHUNK), :] = h2.astype(BF16)
    _for_rows(tm, ROW_CHUNK, chunk)

    lg = jnp.dot(h2_ref[...], wr_ref[...], preferred_element_type=F32) + br_ref[...]
    lane = lax.broadcasted_iota(I32, lg.shape, 1)
    neg = jnp.float32(-jnp.inf)
    big = jnp.int32(ROUTE_LANES)

    def first_max(vals):
        m = jnp.max(vals, axis=-1, keepdims=True)
        idx = jnp.min(jnp.where(vals == m, lane, big), axis=-1, keepdims=True)
        return m, idx

    gl = jnp.where(lane < n_groups, lg, neg)
    gmax, gidx = first_max(gl)
    g_w = 1.0 / jnp.sum(jnp.exp(gl - gmax), axis=-1, keepdims=True)
    lo = n_groups + gidx * per_group
    el = jnp.where((lane >= lo) & (lane < lo + per_group), lg, neg)
    v1, i1 = first_max(el)
    v2, i2 = first_max(jnp.where(lane == i1, neg, el))
    e21 = jnp.exp(v2 - v1)
    w1 = g_w / (1.0 + e21)
    w2 = g_w * e21 / (1.0 + e21)

    hit1 = lane == i1
    hit2 = lane == i2
    onehot = jnp.where(hit1 | hit2, 1.0, 0.0)
    row = lax.broadcasted_iota(I32, (tm, tm), 0)
    col = lax.broadcasted_iota(I32, (tm, tm), 1)
    before = jnp.where(col < row, 1.0, 0.0).astype(BF16)
    earlier = jnp.dot(before, onehot.astype(BF16), preferred_element_type=F32)
    count = jnp.sum(onehot, axis=0, keepdims=True).astype(I32)
    blocks = (count + (SUBLANES - 1)) // SUBLANES
    lrow = lax.broadcasted_iota(I32, (ROUTE_LANES, ROUTE_LANES), 0)
    lcol = lax.broadcasted_iota(I32, (ROUTE_LANES, ROUTE_LANES), 1)
    below = jnp.where(lrow < lcol, 1.0, 0.0).astype(BF16)
    blocks8 = jnp.broadcast_to(blocks.astype(F32), (SUBLANES, ROUTE_LANES)).astype(BF16)
    first = jnp.dot(blocks8, below, preferred_element_type=F32)[0:1] * SUBLANES
    slot = first + earlier
    lpos1 = jnp.sum(jnp.where(hit1, slot, 0.0), axis=-1, keepdims=True)
    lpos2 = jnp.sum(jnp.where(hit2, slot, 0.0), axis=-1, keepdims=True)

    rec = jnp.where(lane == 0, lpos1, jnp.where(lane == 1, lpos2, jnp.where(
        lane == 2, w1, jnp.where(lane == 3, w2, 0.0))))
    lp_ref[...] = rec
    lpt_ref[0] = jnp.transpose(rec)[0:SUBLANES, :]
    sub = lax.broadcasted_iota(I32, (SUBLANES, ROUTE_LANES), 0)
    tab_ref[0] = jnp.where(sub == 0, first.astype(I32), jnp.where(sub == 1, blocks, 0))


def _outproj(yc, ys, wo_bf16, xt, gate1, n2_g, scale2, shift2, wr_bf16, br, n_groups, per_group):
    rows, d = xt.shape
    ch = yc.shape[1]
    tm = min(256, rows)
    row_spec = lambda w: pl.BlockSpec((tm, w), lambda i: (i, 0))
    one = pl.BlockSpec((1, d), lambda i: (0, 0))
    per_b = pl.BlockSpec((SUBLANES, d), lambda i: (0, 0))
    kern = functools.partial(_outproj_kernel, n_groups=n_groups, per_group=per_group)
    return pl.pallas_call(
        kern,
        grid=(rows // tm,),
        in_specs=[row_spec(ch), row_spec(ch),
                  pl.BlockSpec((2, ch, d), lambda i: (0, 0, 0)),
                  row_spec(d), per_b, one, per_b, per_b,
                  pl.BlockSpec((d, ROUTE_LANES), lambda i: (0, 0)),
                  pl.BlockSpec((1, ROUTE_LANES), lambda i: (0, 0))],
        out_specs=[row_spec(d), row_spec(d), row_spec(ROUTE_LANES),
                   pl.BlockSpec((1, SUBLANES, tm), lambda i: (i, 0, 0)),
                   pl.BlockSpec((1, SUBLANES, ROUTE_LANES), lambda i: (i, 0, 0))],
        out_shape=[jax.ShapeDtypeStruct((rows, d), F32),
                   jax.ShapeDtypeStruct((rows, d), BF16),
                   jax.ShapeDtypeStruct((rows, ROUTE_LANES), F32),
                   jax.ShapeDtypeStruct((rows // tm, SUBLANES, tm), F32),
                   jax.ShapeDtypeStruct((rows // tm, SUBLANES, ROUTE_LANES), I32)],
        compiler_params=_params(("arbitrary",)),
    )(yc, ys, wo_bf16, xt, gate1, n2_g.reshape(1, d), scale2, shift2, wr_bf16, br)


def _run_copies(tab_ref, i, n_experts, make_copy):
    def each_block(fn):
        for e in range(n_experts):
            lrow0 = tab_ref[i, e]
            grow0 = tab_ref[i, 2 * n_experts + e]

            def block(c, carry, lrow0=lrow0, grow0=grow0):
                off = c * SUBLANES
                fn(make_copy(pl.multiple_of(lrow0 + off, SUBLANES), pl.multiple_of(grow0 + off, SUBLANES)))
                return carry
            lax.fori_loop(0, tab_ref[i, n_experts + e], block, 0)

    each_block(lambda cp: cp.start())
    each_block(lambda cp: cp.wait())


def _dispatch_kernel(zrow_ref, zflag_ref, nv_ref, tab_ref, lpt_ref, lp_ref, h_ref, hs_ref,
                     sorted_scr, zero_scr, sem, *, n_experts):
    tm, d = h_ref.shape
    te = zero_scr.shape[0]
    n_tiles = hs_ref.shape[0] // te
    n_sorted = sorted_scr.shape[0]
    i = pl.program_id(0)

    @pl.when(pl.program_id(0) == 0)
    def _():
        zero_scr[...] = jnp.zeros(zero_scr.shape, F32)

        def clear(row0):
            cp = pltpu.make_async_copy(zero_scr, hs_ref.at[pl.ds(pl.multiple_of(row0, te), te)], sem)
            cp.start()
            cp.wait()

        for e in range(n_experts):
            @pl.when(zflag_ref[e] > 0)
            def _(e=e):
                clear(zrow_ref[e])

        def clear_tail(t, carry):
            clear(t * te)
            return carry
        lax.fori_loop(nv_ref[0], n_tiles, clear_tail, 0)

    srow = lax.broadcasted_iota(I32, (n_sorted, tm), 0)
    p1 = jnp.where(srow == lpt_ref[0, 0:1, :].astype(I32), 1.0, 0.0)
    p2 = jnp.where(srow == lpt_ref[0, 1:2, :].astype(I32), 1.0, 0.0)
    sorted_scr[:, 0:d] = jnp.dot((p1 + p2).astype(BF16), h_ref[...], preferred_element_type=F32)

    lane = lax.broadcasted_iota(I32, (tm, ROUTE_LANES), 1)

    def pieces(w):
        hi = w.astype(BF16).astype(F32)
        mid = (w - hi).astype(BF16).astype(F32)
        lo = w - hi - mid
        return jnp.where(lane == 0, hi, jnp.where(lane == 1, mid, jnp.where(lane == 2, lo, 0.0))).astype(BF16)

    lp = lp_ref[...]
    sorted_scr[:, d:d + ROUTE_LANES] = (
        jnp.dot(p1.astype(BF16), pieces(lp[:, 2:3]), preferred_element_type=F32)
        + jnp.dot(p2.astype(BF16), pieces(lp[:, 3:4]), preferred_element_type=F32))

    def make_copy(lrow, grow):
        return pltpu.make_async_copy(sorted_scr.at[pl.ds(lrow, SUBLANES)],
                                     hs_ref.at[pl.ds(grow, SUBLANES)], sem)
    _run_copies(tab_ref, i, n_experts, make_copy)


def _dispatch(h2, lp, lpt, tab, zrow, zflag, n_valid, n_slots, tile_e):
    rows, d = h2.shape
    tm = lpt.shape[-1]
    n_experts = zrow.shape[0]
    n_sorted = 2 * tm + n_experts * (SUBLANES - 1)
    n_sorted += -n_sorted % SUBLANES
    width = d + ROUTE_LANES
    grid_spec = pltpu.PrefetchScalarGridSpec(
        num_scalar_prefetch=4,
        grid=(rows // tm,),
        in_specs=[pl.BlockSpec((1, SUBLANES, tm), lambda i, *_: (i, 0, 0)),
                  pl.BlockSpec((tm, ROUTE_LANES), lambda i, *_: (i, 0)),
                  pl.BlockSpec((tm, d), lambda i, *_: (i, 0))],
        out_specs=pl.BlockSpec(memory_space=pl.ANY),
        scratch_shapes=[pltpu.VMEM((n_sorted, width), F32), pltpu.VMEM((tile_e, width), F32),
                        pltpu.SemaphoreType.DMA(())],
    )
    return pl.pallas_call(
        functools.partial(_dispatch_kernel, n_experts=n_experts),
        grid_spec=grid_spec,
        out_shape=jax.ShapeDtypeStruct((n_slots, width), F32),
        compiler_params=_params(("arbitrary",)),
    )(zrow, zflag, n_valid, tab, lpt, lp, h2)


def _expert_kernel(te_ref, nv_ref, hs_ref, wg_ref, wu_ref, wd_ref, o_ref, wg_scr, wu_scr, wd_scr):
    i = pl.program_id(0)

    @pl.when(i < nv_ref[0])
    def _():
        @pl.when((i == 0) | (te_ref[i] != te_ref[jnp.maximum(i - 1, 0)]))
        def _():
            wg_scr[...] = wg_ref[...].astype(BF16)
            wu_scr[...] = wu_ref[...].astype(BF16)
            wd_scr[...] = wd_ref[...].astype(BF16)

        d = wg_scr.shape[0]
        hb = hs_ref[:, 0:d].astype(BF16)
        w = hs_ref[:, d:d + 1] + hs_ref[:, d + 1:d + 2] + hs_ref[:, d + 2:d + 3]
        a = jnp.dot(hb, wg_scr[...], preferred_element_type=F32)
        u = jnp.dot(hb, wu_scr[...], preferred_element_type=F32)
        hm = (a * jax.nn.sigmoid(a) * u * w).astype(BF16)
        o_ref[...] = jnp.dot(hm, wd_scr[...], preferred_element_type=F32)

    @pl.when(i >= nv_ref[0])
    def _():
        o_ref[...] = jnp.zeros(o_ref.shape, F32)


def _experts(hs, w_gate, w_up, w_down, layer, tile_expert, n_valid, tile_e):
    n_slots, width = hs.shape
    d = width - ROUTE_LANES
    per_group = w_gate.shape[2]
    f = w_gate.shape[-1]
    n_tiles = n_slots // tile_e

    def row_map(i, te, nv):
        return (jnp.minimum(i, nv[0] - 1), 0)

    def w_map(i, te, nv):
        e = te[i]
        return (layer, e // per_group, e % per_group, 0, 0)

    grid_spec = pltpu.PrefetchScalarGridSpec(
        num_scalar_prefetch=2,
        grid=(n_tiles,),
        in_specs=[pl.BlockSpec((tile_e, width), row_map),
                  pl.BlockSpec((None, None, None, d, f), w_map),
                  pl.BlockSpec((None, None, None, d, f), w_map),
                  pl.BlockSpec((None, None, None, f, d), w_map)],
        out_specs=pl.BlockSpec((tile_e, d), lambda i, te, nv: (i, 0)),
        scratch_shapes=[pltpu.VMEM((d, f), BF16), pltpu.VMEM((d, f), BF16), pltpu.VMEM((f, d), BF16)],
    )
    return pl.pallas_call(
        _expert_kernel,
        grid_spec=grid_spec,
        out_shape=jax.ShapeDtypeStruct((n_slots, d), F32),
        compiler_params=_params(("arbitrary",)),
    )(tile_expert, n_valid, hs, w_gate, w_up, w_down)


def _combine_kernel(tab_ref, eo_ref, lp_ref, x_ref, g2_ref, o_ref, sorted_scr, y_scr, sem, *, n_experts):
    tm = x_ref.shape[0]
    n_sorted = sorted_scr.shape[0]
    i = pl.program_id(0)

    @pl.when(i == 0)
    def _():
        sorted_scr[pl.ds(2 * tm, n_sorted - 2 * tm), :] = jnp.zeros((n_sorted - 2 * tm, sorted_scr.shape[1]), F32)

    def make_copy(lrow, grow):
        return pltpu.make_async_copy(eo_ref.at[pl.ds(grow, SUBLANES)],
                                     sorted_scr.at[pl.ds(lrow, SUBLANES)], sem)
    _run_copies(tab_ref, i, n_experts, make_copy)

    lp = lp_ref[...]
    scol = lax.broadcasted_iota(I32, (tm, n_sorted), 1)
    q = (jnp.where(scol == lp[:, 0:1].astype(I32), 1.0, 0.0)
         + jnp.where(scol == lp[:, 1:2].astype(I32), 1.0, 0.0)).astype(BF16)
    y_scr[...] = jnp.dot(q, sorted_scr[...].astype(BF16), preferred_element_type=F32)

    g2 = _per_batch(g2_ref[...], ROW_CHUNK)

    def chunk(r0):
        o_ref[pl.ds(r0, ROW_CHUNK), :] = x_ref[pl.ds(r0, ROW_CHUNK), :] + g2 * y_scr[pl.ds(r0, ROW_CHUNK), :]
    _for_rows(tm, ROW_CHUNK, chunk)


def _combine(eo, tab, lp, x1, gate2, n_experts):
    rows, d = x1.shape
    tm = rows // tab.shape[0]
    n_sorted = 2 * tm + n_experts * (SUBLANES - 1)
    n_sorted += -n_sorted % SUBLANES
    grid_spec = pltpu.PrefetchScalarGridSpec(
        num_scalar_prefetch=1,
        grid=(rows // tm,),
        in_specs=[pl.BlockSpec(memory_space=pl.ANY),
                  pl.BlockSpec((tm, ROUTE_LANES), lambda i, *_: (i, 0)),
                  pl.BlockSpec((tm, d), lambda i, *_: (i, 0)),
                  pl.BlockSpec((SUBLANES, d), lambda i, *_: (0, 0))],
        out_specs=pl.BlockSpec((tm, d), lambda i, *_: (i, 0)),
        scratch_shapes=[pltpu.VMEM((n_sorted, d), F32), pltpu.VMEM((tm, d), F32), pltpu.SemaphoreType.DMA(())],
    )
    return pl.pallas_call(
        functools.partial(_combine_kernel, n_experts=n_experts),
        grid_spec=grid_spec,
        out_shape=jax.ShapeDtypeStruct((rows, d), F32),
        compiler_params=_params(("arbitrary",)),
    )(tab, eo, lp, x1, gate2)


def _slots(tabs, rows, n_groups, n_experts, tile_e):
    n_tok_tiles = tabs.shape[0]
    first = tabs[:, 0, n_groups:n_groups + n_experts]
    blocks = tabs[:, 1, n_groups:n_groups + n_experts]
    run = blocks * SUBLANES
    total = jnp.sum(run, axis=0)
    region = ((total + tile_e - 1) // tile_e) * tile_e
    ends = jnp.cumsum(region)
    starts = ends - region
    gfirst = starts[None, :] + jnp.cumsum(run, axis=0) - run
    tab = jnp.concatenate([first, blocks, gfirst], axis=1).astype(I32)
    max_rows = 2 * rows + n_tok_tiles * n_experts * (SUBLANES - 1) + n_experts * (tile_e - SUBLANES)
    n_tiles = max_rows // tile_e
    n_valid = ends[-1] // tile_e
    tile_start = jnp.arange(n_tiles, dtype=I32) * tile_e
    tile_expert = jnp.minimum(jnp.searchsorted(ends, tile_start, side='right'), n_experts - 1).astype(I32)
    last = tile_expert[jnp.maximum(n_valid - 1, 0)]
    tile_expert = jnp.where(jnp.arange(n_tiles) < n_valid, tile_expert, last).astype(I32)
    zrow = jnp.maximum(ends - tile_e, 0).astype(I32)
    zflag = (region > 0).astype(I32)
    return tab, tile_expert, n_valid.reshape(1).astype(I32), zrow, zflag, n_tiles * tile_e


def kernel(x, c, w_ada, b_ada, norm1_g, w_in, conv_w, conv_b, conv_ln_g, conv_ln_b, ssm_a_re, ssm_a_im, ssm_b_re, ssm_b_im, ssm_c_re, ssm_c_im, ssm_d, ssm_log_dt, out_norm_conv_g, out_norm_ssm_g, w_out, norm2_g, router_group_w, router_group_b, router_expert_w, router_expert_b, exp_w_gate, exp_w_up, exp_w_down, final_norm_g):
    bsz, seq, d = x.shape
    assert bsz == SUBLANES
    depth = w_ada.shape[0]
    conv_ch = conv_w.shape[-1]
    ssm_ch = ssm_d.shape[-1]
    assert conv_ch == ssm_ch and w_in.shape[-1] == 2 * conv_ch + 2 * ssm_ch
    n_groups, per_group = router_expert_w.shape[1], router_expert_w.shape[3]
    n_experts = n_groups * per_group
    assert n_groups + n_experts <= ROUTE_LANES
    rows = bsz * seq
    tile_e = min(256, rows // 8)

    mod = _ada(c, w_ada, b_ada)
    abar_re, abar_im, bb_re, bb_im = _ssm_prep(ssm_a_re, ssm_a_im, ssm_log_dt, ssm_b_re, ssm_b_im)
    xt = _to_tb(x)

    for l in range(depth):
        shift1, scale1, gate1, shift2, scale2, gate2 = [mod[l, :, k * d:(k + 1) * d] for k in range(6)]

        p = _inproj(xt, norm1_g[l], scale1, shift1, w_in[l].astype(BF16))
        yc = _conv(p, conv_w[l], conv_b[l], conv_ln_g[l], conv_ln_b[l], out_norm_conv_g[l])
        bc, cc, ar, ai = _ssm_block_weights(abar_re[l], abar_im[l], bb_re[l], bb_im[l], ssm_c_re[l], ssm_c_im[l])
        ys = _ssm(p, 2 * conv_ch // ssm_ch, bc, cc, ar, ai, ssm_d[l], out_norm_ssm_g[l])

        wr = jnp.concatenate([router_group_w[l]] + [router_expert_w[l, g] for g in range(n_groups)], axis=-1)
        wr = jnp.pad(wr, ((0, 0), (0, ROUTE_LANES - wr.shape[1]))).astype(BF16)
        br = jnp.concatenate([router_group_b[l], router_expert_b[l].reshape(-1)])
        br = jnp.pad(br, (0, ROUTE_LANES - br.shape[0])).reshape(1, ROUTE_LANES)
        wo = w_out[l].astype(BF16).reshape(2, conv_ch, d)
        x1, h2, lp, lpt, tabs = _outproj(yc, ys, wo, xt, gate1, norm2_g[l], scale2, shift2, wr, br,
                                         n_groups, per_group)

        tab, tile_expert, n_valid, zrow, zflag, n_slots = _slots(tabs, rows, n_groups, n_experts, tile_e)
        hs = _dispatch(h2, lp, lpt, tab, zrow, zflag, n_valid, n_slots, tile_e)
        eo = _experts(hs, exp_w_gate, exp_w_up, exp_w_down, l, tile_expert, n_valid, tile_e)
        xt = _combine(eo, tab, lp, x1, gate2, n_experts)

    return _final(xt, final_norm_g, bsz)
```

```python
import functools

import jax
import jax.numpy as jnp
from jax import lax
from jax.experimental import pallas as pl
from jax.experimental.pallas import tpu as pltpu

EPS = 1e-6
F32 = jnp.float32
BF16 = jnp.bfloat16
I32 = jnp.int32

SUBLANES = 8
LANES = 128
MXU_DIM = 256
VMEM_LIMIT = 56 * 1024 * 1024
ROUTE_LANES = LANES
ROW_CHUNK = 16
CONV_ROWS = 32
CONV_LANES = 512
EXPERT_TILE = 512


def _params(sem):
    return pltpu.CompilerParams(dimension_semantics=sem, vmem_limit_bytes=VMEM_LIMIT)


def _for_rows(n_rows, chunk, body, unroll=2):
    def step(i, carry):
        body(pl.multiple_of(i * chunk, chunk))
        return carry
    lax.fori_loop(0, n_rows // chunk, step, 0, unroll=unroll)


def _per_batch(v, rows):
    return jnp.tile(v, (rows // SUBLANES, 1))


def _ada_kernel(c_ref, w_ref, b_ref, o_ref):
    c = c_ref[...]
    ca = (c * jax.nn.sigmoid(c)).astype(BF16)
    o_ref[...] = jnp.dot(ca, w_ref[...].astype(BF16), preferred_element_type=F32) + b_ref[...]


def _ada(c, w_ada, b_ada):
    depth, d, n = w_ada.shape
    bsz = c.shape[0]
    tn = min(512, n)
    return pl.pallas_call(
        _ada_kernel,
        grid=(depth, n // tn),
        in_specs=[pl.BlockSpec((bsz, d), lambda l, j: (0, 0)),
                  pl.BlockSpec((None, d, tn), lambda l, j: (l, 0, j)),
                  pl.BlockSpec((None, 1, tn), lambda l, j: (l, 0, j))],
        out_specs=pl.BlockSpec((None, bsz, tn), lambda l, j: (l, 0, j)),
        out_shape=jax.ShapeDtypeStruct((depth, bsz, n), F32),
        compiler_params=_params(("arbitrary", "arbitrary")),
    )(c, w_ada, b_ada.reshape(depth, 1, n))


def _inproj_kernel(x_ref, g_ref, sc_ref, sh_ref, w_ref, *rest, tn, batch_major):
    if batch_major:
        o_ref, xt_ref, h_scr = rest
        xt_ref[...] = jnp.swapaxes(x_ref[...], 0, 1).reshape(xt_ref.shape)
        x_ref = xt_ref
    else:
        o_ref, h_scr = rest
    tm = x_ref.shape[0]
    g = g_ref[...]
    sc = _per_batch(1.0 + sc_ref[...], ROW_CHUNK)
    sh = _per_batch(sh_ref[...], ROW_CHUNK)

    def chunk(r0):
        x = x_ref[pl.ds(r0, ROW_CHUNK), :]
        ms = jnp.mean(x * x, axis=-1, keepdims=True)
        h = x * lax.rsqrt(ms + EPS) * g
        h_scr[pl.ds(r0, ROW_CHUNK), :] = (h * sc + sh).astype(BF16)
    _for_rows(tm, ROW_CHUNK, chunk, unroll=4)

    for n0 in range(0, w_ref.shape[1], tn):
        o_ref[:, n0:n0 + tn] = jnp.dot(
            h_scr[...], w_ref[:, n0:n0 + tn], preferred_element_type=F32).astype(o_ref.dtype)


def _inproj(x, g, scale, shift, w_bf16):
    batch_major = x.ndim == 3
    d = x.shape[-1]
    rows = x.size // d
    n = w_bf16.shape[1]
    tm = min(512, rows)
    tn = min(1024, n)
    row_spec = lambda w: pl.BlockSpec((tm, w), lambda i: (i, 0))
    if batch_major:
        x_spec = pl.BlockSpec((SUBLANES, tm // SUBLANES, d), lambda i: (0, i, 0))
        out_specs = [row_spec(n), row_spec(d)]
        out_shape = [jax.ShapeDtypeStruct((rows, n), BF16), jax.ShapeDtypeStruct((rows, d), F32)]
    else:
        x_spec = row_spec(d)
        out_specs = row_spec(n)
        out_shape = jax.ShapeDtypeStruct((rows, n), BF16)
    return pl.pallas_call(
        functools.partial(_inproj_kernel, tn=tn, batch_major=batch_major),
        grid=(rows // tm,),
        in_specs=[x_spec,
                  pl.BlockSpec((1, d), lambda i: (0, 0)),
                  pl.BlockSpec((SUBLANES, d), lambda i: (0, 0)),
                  pl.BlockSpec((SUBLANES, d), lambda i: (0, 0)),
                  pl.BlockSpec((d, n), lambda i: (0, 0))],
        out_specs=out_specs,
        out_shape=out_shape,
        scratch_shapes=[pltpu.VMEM((tm, d), BF16)],
        compiler_params=_params(("arbitrary",)),
    )(x, g.reshape(1, d), scale, shift, w_bf16)


def _conv_kernel(v_ref, gt_ref, w_ref, cb_ref, lg_ref, lb_ref, og_ref, o_ref, z_scr, a_scr, *, width):
    rows, ch = v_ref.shape
    halo = (width - 1) * SUBLANES

    @pl.when(pl.program_id(0) == 0)
    def _():
        z_scr[pl.ds(0, halo), :] = jnp.zeros((halo, ch), F32)

    def glu(r0):
        v = v_ref[pl.ds(r0, ROW_CHUNK), :].astype(F32)
        gt = gt_ref[pl.ds(r0, ROW_CHUNK), :].astype(F32)
        z_scr[pl.ds(halo + r0, ROW_CHUNK), :] = v * jax.nn.sigmoid(gt)
    _for_rows(rows, ROW_CHUNK, glu)

    lg = lg_ref[...]
    lb = lb_ref[...]
    og = og_ref[...]
    reps = CONV_ROWS // SUBLANES
    lanes = min(CONV_LANES, ch)

    def taps(r0):
        for c0 in range(0, ch, lanes):
            cb = cb_ref[:, c0:c0 + lanes]
            acc = [cb] * reps
            w = {}
            for j in range(reps + width - 1):
                z = z_scr[pl.ds(r0 + j * SUBLANES, SUBLANES), c0:c0 + lanes]
                if j < width:
                    w[j] = w_ref[pl.ds(j * SUBLANES, SUBLANES), c0:c0 + lanes]
                for m in range(reps):
                    k = j - m
                    if 0 <= k < width:
                        acc[m] = acc[m] + w[k] * z
            for m in range(reps):
                a_scr[pl.ds(r0 + m * SUBLANES, SUBLANES), c0:c0 + lanes] = acc[m]
    _for_rows(rows, CONV_ROWS, taps, unroll=1)

    def conv(r0):
        acc = a_scr[pl.ds(r0, ROW_CHUNK), :]
        mu = jnp.mean(acc, axis=-1, keepdims=True)
        cen = acc - mu
        var = jnp.mean(cen * cen, axis=-1, keepdims=True)
        y = cen * lax.rsqrt(var + EPS) * lg + lb
        s = y * jax.nn.sigmoid(y)
        ms = jnp.mean(s * s, axis=-1, keepdims=True)
        o_ref[pl.ds(r0, ROW_CHUNK), :] = (s * lax.rsqrt(ms + EPS) * og).astype(BF16)
    _for_rows(rows, ROW_CHUNK, conv, unroll=4)

    z_scr[pl.ds(0, halo), :] = z_scr[pl.ds(rows, halo), :]


def _conv(p, conv_w, conv_b, ln_g, ln_b, out_g):
    rows = p.shape[0]
    width, ch = conv_w.shape
    halo = (width - 1) * SUBLANES
    tr = min(512, rows)
    assert tr >= halo
    vec = lambda a: a.reshape(1, ch)
    row_spec = lambda col: pl.BlockSpec((tr, ch), lambda i, col=col: (i, col))
    one = pl.BlockSpec((1, ch), lambda i: (0, 0))
    w8 = jnp.broadcast_to(conv_w[:, None, :], (width, SUBLANES, ch)).reshape(width * SUBLANES, ch)
    cb8 = jnp.broadcast_to(conv_b[None, :], (SUBLANES, ch))
    return pl.pallas_call(
        functools.partial(_conv_kernel, width=width),
        grid=(rows // tr,),
        in_specs=[row_spec(0), row_spec(1),
                  pl.BlockSpec((width * SUBLANES, ch), lambda i: (0, 0)),
                  pl.BlockSpec((SUBLANES, ch), lambda i: (0, 0)), one, one, one],
        out_specs=pl.BlockSpec((tr, ch), lambda i: (i, 0)),
        out_shape=jax.ShapeDtypeStruct((rows, ch), BF16),
        scratch_shapes=[pltpu.VMEM((halo + tr, ch), F32), pltpu.VMEM((tr, ch), F32)],
        compiler_params=_params(("arbitrary",)),
    )(p, p, w8, cb8, vec(ln_g), vec(ln_b), vec(out_g))


def _ssm_prep_kernel(lr_ref, li_ref, ldt_ref, br_ref, bi_ref, ar_o, ai_o, bbr_o, bbi_o):
    lr = lr_ref[...]
    li = li_ref[...]
    dt = jnp.exp(ldt_ref[...])
    mag = jnp.exp(lr * dt)
    ang = li * dt
    ar = mag * jnp.cos(ang)
    ai = mag * jnp.sin(ang)
    den = lr * lr + li * li
    nr = ar - 1.0
    ni = ai
    fr = (nr * lr + ni * li) / den
    fi = (ni * lr - nr * li) / den
    br = br_ref[...]
    bi = bi_ref[...]
    ar_o[...] = ar
    ai_o[...] = ai
    bbr_o[...] = fr * br - fi * bi
    bbi_o[...] = fr * bi + fi * br


def _ssm_prep(a_re, a_im, log_dt, b_re, b_im):
    depth, g, p = a_re.shape
    h = b_re.shape[-1]
    rep = lambda a: jnp.broadcast_to(a[:, :, None, :], (depth, g, h, p)).reshape(depth * g * h, p)
    ldt = jnp.broadcast_to(log_dt[:, :, None, None], (depth, g, h, p)).reshape(depth * g * h, p)
    bt = lambda b: jnp.swapaxes(b, -1, -2).reshape(depth * g * h, p)
    n = g * h
    spec = pl.BlockSpec((n, p), lambda l: (l, 0))
    shp = jax.ShapeDtypeStruct((depth * n, p), F32)
    ar, ai, bbr, bbi = pl.pallas_call(
        _ssm_prep_kernel,
        grid=(depth,),
        in_specs=[spec] * 5,
        out_specs=[spec] * 4,
        out_shape=[shp] * 4,
        compiler_params=_params(("arbitrary",)),
    )(rep(a_re), rep(a_im), ldt, bt(b_re), bt(b_im))
    r4 = lambda a: a.reshape(depth, g, h, p)
    return r4(ar)[:, :, 0, :], r4(ai)[:, :, 0, :], r4(bbr), r4(bbi)


def _ssm_block_weights(abar_re, abar_im, bb_re, bb_im, c_re, c_im):
    g, h, p = bb_re.shape
    gpt = min(MXU_DIM // h, g)
    kt = g // gpt
    eye = jnp.eye(gpt, dtype=F32)

    def bmat(bb):
        b4 = bb.reshape(kt, gpt, h, p)
        return jnp.einsum('kghp,gG->kghGp', b4, eye).reshape(kt, gpt * h, gpt * p)

    def cmat(c):
        c4 = c.reshape(kt, gpt, h, p)
        return jnp.einsum('kghp,gG->kgpGh', c4, eye).reshape(kt, gpt * p, gpt * h)

    bc = jnp.concatenate([bmat(bb_re), bmat(bb_im)], axis=-1).astype(BF16)
    cc = jnp.concatenate([cmat(c_re), -cmat(c_im)], axis=1).astype(BF16)
    bcast = lambda a: jnp.broadcast_to(a.reshape(1, g * p), (SUBLANES, g * p))
    return bc, cc, bcast(abar_re), bcast(abar_im)


def _ssm_kernel(u_ref, gt_ref, bc_ref, cc_ref, ar_ref, ai_ref, d_ref, og_ref, o_ref,
                s_scr, st_scr, y_scr, *, lane_chunk):
    rows, ch = u_ref.shape
    kt, kch, sw2 = bc_ref.shape
    sw = sw2 // 2
    steps = rows // SUBLANES

    @pl.when(pl.program_id(0) == 0)
    def _():
        st_scr[...] = jnp.zeros(st_scr.shape, F32)

    for k in range(kt):
        ub = u_ref[:, k * kch:(k + 1) * kch].astype(BF16)
        s_scr[:, k * sw2:(k + 1) * sw2] = jnp.dot(ub, bc_ref[k], preferred_element_type=F32)

    for k in range(kt):
        for c in range(sw // lane_chunk):
            re0 = k * sw2 + c * lane_chunk
            im0 = re0 + sw
            a0 = k * sw + c * lane_chunk
            ar = ar_ref[:, a0:a0 + lane_chunk]
            ai = ai_ref[:, a0:a0 + lane_chunk]

            def step(t, carry, re0=re0, im0=im0, ar=ar, ai=ai):
                sr, si = carry
                r = pl.multiple_of(t * SUBLANES, SUBLANES)
                br = s_scr[pl.ds(r, SUBLANES), re0:re0 + lane_chunk]
                bi = s_scr[pl.ds(r, SUBLANES), im0:im0 + lane_chunk]
                nr = ar * sr - ai * si + br
                ni = ar * si + ai * sr + bi
                s_scr[pl.ds(r, SUBLANES), re0:re0 + lane_chunk] = nr
                s_scr[pl.ds(r, SUBLANES), im0:im0 + lane_chunk] = ni
                return nr, ni

            sr, si = lax.fori_loop(
                0, steps, step,
                (st_scr[:, re0:re0 + lane_chunk], st_scr[:, im0:im0 + lane_chunk]))
            st_scr[:, re0:re0 + lane_chunk] = sr
            st_scr[:, im0:im0 + lane_chunk] = si

    for k in range(kt):
        sb = s_scr[:, k * sw2:(k + 1) * sw2].astype(BF16)
        y_scr[:, k * kch:(k + 1) * kch] = jnp.dot(sb, cc_ref[k], preferred_element_type=F32)

    d = d_ref[...]
    og = og_ref[...]

    def gate(r0):
        u = u_ref[pl.ds(r0, ROW_CHUNK), :].astype(F32)
        y = y_scr[pl.ds(r0, ROW_CHUNK), :] + u * d
        v = jax.nn.gelu(y) * jax.nn.sigmoid(gt_ref[pl.ds(r0, ROW_CHUNK), :].astype(F32))
        ms = jnp.mean(v * v, axis=-1, keepdims=True)
        o_ref[pl.ds(r0, ROW_CHUNK), :] = (v * lax.rsqrt(ms + EPS) * og).astype(BF16)
    _for_rows(rows, ROW_CHUNK, gate, unroll=4)


def _ssm(p, col0, bc, cc, ar, ai, d_skip, out_g):
    rows = p.shape[0]
    kt, kch, sw2 = bc.shape
    ch = kt * kch
    tr = min(256, rows)
    lane_chunk = min(512, sw2 // 2)
    vec = lambda a: a.reshape(1, ch)
    row_spec = lambda col: pl.BlockSpec((tr, ch), lambda i, col=col: (i, col))
    one = pl.BlockSpec((1, ch), lambda i: (0, 0))
    full = lambda a: pl.BlockSpec(a.shape, lambda i, nd=a.ndim: (0,) * nd)
    return pl.pallas_call(
        functools.partial(_ssm_kernel, lane_chunk=lane_chunk),
        grid=(rows // tr,),
        in_specs=[row_spec(col0), row_spec(col0 + 1), full(bc), full(cc), full(ar), full(ai), one, one],
        out_specs=pl.BlockSpec((tr, ch), lambda i: (i, 0)),
        out_shape=jax.ShapeDtypeStruct((rows, ch), BF16),
        scratch_shapes=[pltpu.VMEM((tr, kt * sw2), F32),
                        pltpu.VMEM((SUBLANES, kt * sw2), F32),
                        pltpu.VMEM((tr, ch), F32)],
        compiler_params=_params(("arbitrary",)),
    )(p, p, bc, cc, ar, ai, vec(d_skip), vec(out_g))


def _outproj_kernel(yc_ref, ys_ref, wo_ref, x_ref, g1_ref, n2_ref, sc_ref, sh_ref, wr_ref, br_ref,
                    x1_ref, h2_ref, lp_ref, lpt_ref, tab_ref, *, n_groups, per_group):
    tm = x_ref.shape[0]

    x1_ref[...] = (jnp.dot(yc_ref[...], wo_ref[0], preferred_element_type=F32)
                   + jnp.dot(ys_ref[...], wo_ref[1], preferred_element_type=F32))

    g1 = _per_batch(g1_ref[...], ROW_CHUNK)
    sc = _per_batch(1.0 + sc_ref[...], ROW_CHUNK)
    sh = _per_batch(sh_ref[...], ROW_CHUNK)
    n2 = n2_ref[...]

    def chunk(r0):
        x1 = x_ref[pl.ds(r0, ROW_CHUNK), :] + g1 * x1_ref[pl.ds(r0, ROW_CHUNK), :]
        x1_ref[pl.ds(r0, ROW_CHUNK), :] = x1
        ms = jnp.mean(x1 * x1, axis=-1, keepdims=True)
        h2 = x1 * lax.rsqrt(ms + EPS) * n2 * sc + sh
        h2_ref[pl.ds(r0, ROW_CHUNK), :] = h2.astype(BF16)
    _for_rows(tm, ROW_CHUNK, chunk, unroll=4)

    lg = jnp.dot(h2_ref[...], wr_ref[...], preferred_element_type=F32) + br_ref[...]
    lane = lax.broadcasted_iota(I32, lg.shape, 1)
    neg = jnp.float32(-jnp.inf)
    big = jnp.int32(ROUTE_LANES)

    def first_max(vals):
        m = jnp.max(vals, axis=-1, keepdims=True)
        idx = jnp.min(jnp.where(vals == m, lane, big), axis=-1, keepdims=True)
        return m, idx

    gl = jnp.where(lane < n_groups, lg, neg)
    gmax, gidx = first_max(gl)
    g_w = 1.0 / jnp.sum(jnp.exp(gl - gmax), axis=-1, keepdims=True)
    lo = n_groups + gidx * per_group
    el = jnp.where((lane >= lo) & (lane < lo + per_group), lg, neg)
    v1, i1 = first_max(el)
    v2, i2 = first_max(jnp.where(lane == i1, neg, el))
    e21 = jnp.exp(v2 - v1)
    w1 = g_w / (1.0 + e21)
    w2 = g_w * e21 / (1.0 + e21)

    hit1 = lane == i1
    hit2 = lane == i2
    onehot = jnp.where(hit1 | hit2, 1.0, 0.0)
    row = lax.broadcasted_iota(I32, (tm, tm), 0)
    col = lax.broadcasted_iota(I32, (tm, tm), 1)
    before = jnp.where(col < row, 1.0, 0.0).astype(BF16)
    earlier = jnp.dot(before, onehot.astype(BF16), preferred_element_type=F32)
    count = jnp.sum(onehot, axis=0, keepdims=True).astype(I32)
    blocks = (count + (SUBLANES - 1)) // SUBLANES
    lrow = lax.broadcasted_iota(I32, (ROUTE_LANES, ROUTE_LANES), 0)
    lcol = lax.broadcasted_iota(I32, (ROUTE_LANES, ROUTE_LANES), 1)
    below = jnp.where(lrow < lcol, 1.0, 0.0).astype(BF16)
    blocks8 = jnp.broadcast_to(blocks.astype(F32), (SUBLANES, ROUTE_LANES)).astype(BF16)
    first = jnp.dot(blocks8, below, preferred_element_type=F32)[0:1] * SUBLANES
    slot = first + earlier
    lpos1 = jnp.sum(jnp.where(hit1, slot, 0.0), axis=-1, keepdims=True)
    lpos2 = jnp.sum(jnp.where(hit2, slot, 0.0), axis=-1, keepdims=True)

    rec = jnp.where(lane == 0, lpos1, jnp.where(lane == 1, lpos2, jnp.where(
        lane == 2, w1, jnp.where(lane == 3, w2, 0.0))))
    lp_ref[...] = rec
    lpt_ref[0] = jnp.transpose(rec)[0:SUBLANES, :]
    sub = lax.broadcasted_iota(I32, (SUBLANES, ROUTE_LANES), 0)
    tab_ref[0] = jnp.where(sub == 0, first.astype(I32), jnp.where(sub == 1, blocks, 0))


def _outproj(yc, ys, wo_bf16, xt, gate1, n2_g, scale2, shift2, wr_bf16, br, n_groups, per_group):
    rows, d = xt.shape
    ch = yc.shape[1]
    tm = min(256, rows)
    row_spec = lambda w: pl.BlockSpec((tm, w), lambda i: (i, 0))
    one = pl.BlockSpec((1, d), lambda i: (0, 0))
    per_b = pl.BlockSpec((SUBLANES, d), lambda i: (0, 0))
    kern = functools.partial(_outproj_kernel, n_groups=n_groups, per_group=per_group)
    return pl.pallas_call(
        kern,
        grid=(rows // tm,),
        in_specs=[row_spec(ch), row_spec(ch),
                  pl.BlockSpec((2, ch, d), lambda i: (0, 0, 0)),
                  row_spec(d), per_b, one, per_b, per_b,
                  pl.BlockSpec((d, ROUTE_LANES), lambda i: (0, 0)),
                  pl.BlockSpec((1, ROUTE_LANES), lambda i: (0, 0))],
        out_specs=[row_spec(d), row_spec(d), row_spec(ROUTE_LANES),
                   pl.BlockSpec((1, SUBLANES, tm), lambda i: (i, 0, 0)),
                   pl.BlockSpec((1, SUBLANES, ROUTE_LANES), lambda i: (i, 0, 0))],
        out_shape=[jax.ShapeDtypeStruct((rows, d), F32),
                   jax.ShapeDtypeStruct((rows, d), BF16),
                   jax.ShapeDtypeStruct((rows, ROUTE_LANES), F32),
                   jax.ShapeDtypeStruct((rows // tm, SUBLANES, tm), F32),
                   jax.ShapeDtypeStruct((rows // tm, SUBLANES, ROUTE_LANES), I32)],
        compiler_params=_params(("arbitrary",)),
    )(yc, ys, wo_bf16, xt, gate1, n2_g.reshape(1, d), scale2, shift2, wr_bf16, br)


def _for_blocks(tab_ref, tile, n_experts, fn):
    for e in range(n_experts):
        lrow0 = tab_ref[tile, e]
        grow0 = tab_ref[tile, 2 * n_experts + e]

        def block(c, carry, lrow0=lrow0, grow0=grow0):
            off = c * SUBLANES
            fn(pl.multiple_of(lrow0 + off, SUBLANES), pl.multiple_of(grow0 + off, SUBLANES))
            return carry
        lax.fori_loop(0, tab_ref[tile, n_experts + e], block, 0)


def _dispatch_kernel(zrow_ref, zflag_ref, nv_ref, tab_ref, lpt_ref, lp_ref, h_ref, hs_ref,
                     sorted_scr, zero_scr, sem, zsem, *, n_experts):
    tm, d = h_ref.shape
    te = zero_scr.shape[0]
    n_tiles = hs_ref.shape[0] // te
    n_sorted = sorted_scr.shape[1]
    i = pl.program_id(0)
    slot = i % 2

    @pl.when(i == 0)
    def _():
        zero_scr[...] = jnp.zeros(zero_scr.shape, F32)

        def clear(row0):
            return pltpu.make_async_copy(zero_scr, hs_ref.at[pl.ds(pl.multiple_of(row0, te), te)], zsem)

        def each_clear(fn):
            for e in range(n_experts):
                @pl.when(zflag_ref[e] > 0)
                def _(e=e):
                    fn(clear(zrow_ref[e]))

            def tail(t, carry):
                fn(clear(t * te))
                return carry
            lax.fori_loop(nv_ref[0], n_tiles, tail, 0)

        each_clear(lambda cp: cp.start())
        each_clear(lambda cp: cp.wait())

    srow = lax.broadcasted_iota(I32, (n_sorted, tm), 0)
    p1 = jnp.where(srow == lpt_ref[0, 0:1, :].astype(I32), 1.0, 0.0)
    p2 = jnp.where(srow == lpt_ref[0, 1:2, :].astype(I32), 1.0, 0.0)
    sorted_scr[slot, :, 0:d] = jnp.dot((p1 + p2).astype(BF16), h_ref[...], preferred_element_type=F32)

    lane = lax.broadcasted_iota(I32, (tm, ROUTE_LANES), 1)

    def pieces(w):
        hi = w.astype(BF16).astype(F32)
        mid = (w - hi).astype(BF16).astype(F32)
        lo = w - hi - mid
        return jnp.where(lane == 0, hi, jnp.where(lane == 1, mid, jnp.where(lane == 2, lo, 0.0))).astype(BF16)

    lp = lp_ref[...]
    sorted_scr[slot, :, d:d + ROUTE_LANES] = (
        jnp.dot(p1.astype(BF16), pieces(lp[:, 2:3]), preferred_element_type=F32)
        + jnp.dot(p2.astype(BF16), pieces(lp[:, 3:4]), preferred_element_type=F32))

    def copy(buf):
        return lambda lrow, grow: pltpu.make_async_copy(
            sorted_scr.at[buf, pl.ds(lrow, SUBLANES)], hs_ref.at[pl.ds(grow, SUBLANES)], sem.at[buf])

    _for_blocks(tab_ref, i, n_experts, lambda lrow, grow: copy(slot)(lrow, grow).start())

    @pl.when(i > 0)
    def _():
        _for_blocks(tab_ref, i - 1, n_experts, lambda lrow, grow: copy(1 - slot)(lrow, grow).wait())

    @pl.when(i == pl.num_programs(0) - 1)
    def _():
        _for_blocks(tab_ref, i, n_experts, lambda lrow, grow: copy(slot)(lrow, grow).wait())


def _dispatch(h2, lp, lpt, tab, zrow, zflag, n_valid, n_slots, tile_e):
    rows, d = h2.shape
    tm = lpt.shape[-1]
    n_experts = zrow.shape[0]
    n_sorted = 2 * tm + n_experts * (SUBLANES - 1)
    n_sorted += -n_sorted % SUBLANES
    width = d + ROUTE_LANES
    grid_spec = pltpu.PrefetchScalarGridSpec(
        num_scalar_prefetch=4,
        grid=(rows // tm,),
        in_specs=[pl.BlockSpec((1, SUBLANES, tm), lambda i, *_: (i, 0, 0)),
                  pl.BlockSpec((tm, ROUTE_LANES), lambda i, *_: (i, 0)),
                  pl.BlockSpec((tm, d), lambda i, *_: (i, 0))],
        out_specs=pl.BlockSpec(memory_space=pl.ANY),
        scratch_shapes=[pltpu.VMEM((2, n_sorted, width), F32), pltpu.VMEM((tile_e, width), F32),
                        pltpu.SemaphoreType.DMA((2,)), pltpu.SemaphoreType.DMA(())],
    )
    return pl.pallas_call(
        functools.partial(_dispatch_kernel, n_experts=n_experts),
        grid_spec=grid_spec,
        out_shape=jax.ShapeDtypeStruct((n_slots, width), F32),
        compiler_params=_params(("arbitrary",)),
    )(zrow, zflag, n_valid, tab, lpt, lp, h2)


def _expert_kernel(te_ref, nv_ref, hs_ref, wg_ref, wu_ref, wd_ref, o_ref, wg_scr, wu_scr, wd_scr):
    i = pl.program_id(0)

    @pl.when(i < nv_ref[0])
    def _():
        @pl.when((i == 0) | (te_ref[i] != te_ref[jnp.maximum(i - 1, 0)]))
        def _():
            wg_scr[...] = wg_ref[...].astype(BF16)
            wu_scr[...] = wu_ref[...].astype(BF16)
            wd_scr[...] = wd_ref[...].astype(BF16)

        d = wg_scr.shape[0]
        hb = hs_ref[:, 0:d].astype(BF16)
        w = hs_ref[:, d:d + 1] + hs_ref[:, d + 1:d + 2] + hs_ref[:, d + 2:d + 3]
        a = jnp.dot(hb, wg_scr[...], preferred_element_type=F32)
        u = jnp.dot(hb, wu_scr[...], preferred_element_type=F32)
        hm = (a * jax.nn.sigmoid(a) * u * w).astype(BF16)
        o_ref[...] = jnp.dot(hm, wd_scr[...], preferred_element_type=F32)

    @pl.when(i >= nv_ref[0])
    def _():
        o_ref[...] = jnp.zeros(o_ref.shape, F32)


def _experts(hs, w_gate, w_up, w_down, layer, tile_expert, n_valid, tile_e):
    n_slots, width = hs.shape
    d = width - ROUTE_LANES
    per_group = w_gate.shape[2]
    f = w_gate.shape[-1]
    n_tiles = n_slots // tile_e

    def row_map(i, te, nv):
        return (jnp.minimum(i, nv[0] - 1), 0)

    def w_map(i, te, nv):
        e = te[i]
        return (layer, e // per_group, e % per_group, 0, 0)

    grid_spec = pltpu.PrefetchScalarGridSpec(
        num_scalar_prefetch=2,
        grid=(n_tiles,),
        in_specs=[pl.BlockSpec((tile_e, width), row_map),
                  pl.BlockSpec((None, None, None, d, f), w_map),
                  pl.BlockSpec((None, None, None, d, f), w_map),
                  pl.BlockSpec((None, None, None, f, d), w_map)],
        out_specs=pl.BlockSpec((tile_e, d), lambda i, te, nv: (i, 0)),
        scratch_shapes=[pltpu.VMEM((d, f), BF16), pltpu.VMEM((d, f), BF16), pltpu.VMEM((f, d), BF16)],
    )
    return pl.pallas_call(
        _expert_kernel,
        grid_spec=grid_spec,
        out_shape=jax.ShapeDtypeStruct((n_slots, d), F32),
        compiler_params=_params(("arbitrary",)),
    )(tile_expert, n_valid, hs, w_gate, w_up, w_down)


def _combine_kernel(tab_ref, eo_ref, lp_ref, x_ref, g2_ref, gf_ref, o_ref, sorted_scr, y_scr, sem,
                    *, n_experts, final):
    tm, d = x_ref.shape
    n_sorted = sorted_scr.shape[1]
    i = pl.program_id(0)
    slot = i % 2

    def fetch(buf):
        return lambda lrow, grow: pltpu.make_async_copy(
            eo_ref.at[pl.ds(grow, SUBLANES)], sorted_scr.at[buf, pl.ds(lrow, SUBLANES)], sem.at[buf])

    @pl.when(i == 0)
    def _():
        for buf in range(2):
            sorted_scr[buf, pl.ds(2 * tm, n_sorted - 2 * tm), :] = jnp.zeros((n_sorted - 2 * tm, d), F32)
        _for_blocks(tab_ref, 0, n_experts, lambda lrow, grow: fetch(0)(lrow, grow).start())

    @pl.when(i + 1 < pl.num_programs(0))
    def _():
        _for_blocks(tab_ref, i + 1, n_experts, lambda lrow, grow: fetch(1 - slot)(lrow, grow).start())

    _for_blocks(tab_ref, i, n_experts, lambda lrow, grow: fetch(slot)(lrow, grow).wait())

    lp = lp_ref[...]
    scol = lax.broadcasted_iota(I32, (tm, n_sorted), 1)
    q = (jnp.where(scol == lp[:, 0:1].astype(I32), 1.0, 0.0)
         + jnp.where(scol == lp[:, 1:2].astype(I32), 1.0, 0.0)).astype(BF16)
    y_scr[...] = jnp.dot(q, sorted_scr[slot].astype(BF16), preferred_element_type=F32)

    g2 = _per_batch(g2_ref[...], ROW_CHUNK)
    gf = gf_ref[...]

    def chunk(r0):
        x2 = x_ref[pl.ds(r0, ROW_CHUNK), :] + g2 * y_scr[pl.ds(r0, ROW_CHUNK), :]
        if final:
            ms = jnp.mean(x2 * x2, axis=-1, keepdims=True)
            y_scr[pl.ds(r0, ROW_CHUNK), :] = x2 * lax.rsqrt(ms + EPS) * gf
        else:
            o_ref[pl.ds(r0, ROW_CHUNK), :] = x2
    _for_rows(tm, ROW_CHUNK, chunk, unroll=4)

    if final:
        o_ref[...] = jnp.swapaxes(y_scr[...].reshape(tm // SUBLANES, SUBLANES, d), 0, 1)


def _combine(eo, tab, lp, x1, gate2, final_g, n_experts, final):
    rows, d = x1.shape
    tm = rows // tab.shape[0]
    n_sorted = 2 * tm + n_experts * (SUBLANES - 1)
    n_sorted += -n_sorted % SUBLANES
    if final:
        out_spec = pl.BlockSpec((SUBLANES, tm // SUBLANES, d), lambda i, *_: (0, i, 0))
        out_shape = jax.ShapeDtypeStruct((SUBLANES, rows // SUBLANES, d), F32)
    else:
        out_spec = pl.BlockSpec((tm, d), lambda i, *_: (i, 0))
        out_shape = jax.ShapeDtypeStruct((rows, d), F32)
    grid_spec = pltpu.PrefetchScalarGridSpec(
        num_scalar_prefetch=1,
        grid=(rows // tm,),
        in_specs=[pl.BlockSpec(memory_space=pl.ANY),
                  pl.BlockSpec((tm, ROUTE_LANES), lambda i, *_: (i, 0)),
                  pl.BlockSpec((tm, d), lambda i, *_: (i, 0)),
                  pl.BlockSpec((SUBLANES, d), lambda i, *_: (0, 0)),
                  pl.BlockSpec((1, d), lambda i, *_: (0, 0))],
        out_specs=out_spec,
        scratch_shapes=[pltpu.VMEM((2, n_sorted, d), F32), pltpu.VMEM((tm, d), F32),
                        pltpu.SemaphoreType.DMA((2,))],
    )
    return pl.pallas_call(
        functools.partial(_combine_kernel, n_experts=n_experts, final=final),
        grid_spec=grid_spec,
        out_shape=out_shape,
        compiler_params=_params(("arbitrary",)),
    )(tab, eo, lp, x1, gate2, final_g.reshape(1, d))


def _slots(tabs, rows, n_groups, n_experts, tile_e):
    n_tok_tiles = tabs.shape[0]
    first = tabs[:, 0, n_groups:n_groups + n_experts]
    blocks = tabs[:, 1, n_groups:n_groups + n_experts]
    run = blocks * SUBLANES
    total = jnp.sum(run, axis=0)
    region = ((total + tile_e - 1) // tile_e) * tile_e
    ends = jnp.cumsum(region)
    starts = ends - region
    gfirst = starts[None, :] + jnp.cumsum(run, axis=0) - run
    tab = jnp.concatenate([first, blocks, gfirst], axis=1).astype(I32)
    max_rows = 2 * rows + n_tok_tiles * n_experts * (SUBLANES - 1) + n_experts * (tile_e - SUBLANES)
    n_tiles = max_rows // tile_e
    n_valid = ends[-1] // tile_e
    tile_start = jnp.minimum(jnp.arange(n_tiles, dtype=I32), jnp.maximum(n_valid - 1, 0)) * tile_e
    tile_expert = jnp.sum((ends[None, :] <= tile_start[:, None]).astype(I32), axis=1)
    tile_expert = jnp.minimum(tile_expert, n_experts - 1).astype(I32)
    zrow = jnp.maximum(ends - tile_e, 0).astype(I32)
    zflag = (region > 0).astype(I32)
    return tab, tile_expert, n_valid.reshape(1).astype(I32), zrow, zflag, n_tiles * tile_e


def kernel(x, c, w_ada, b_ada, norm1_g, w_in, conv_w, conv_b, conv_ln_g, conv_ln_b, ssm_a_re, ssm_a_im, ssm_b_re, ssm_b_im, ssm_c_re, ssm_c_im, ssm_d, ssm_log_dt, out_norm_conv_g, out_norm_ssm_g, w_out, norm2_g, router_group_w, router_group_b, router_expert_w, router_expert_b, exp_w_gate, exp_w_up, exp_w_down, final_norm_g):
    bsz, seq, d = x.shape
    assert bsz == SUBLANES
    depth = w_ada.shape[0]
    conv_ch = conv_w.shape[-1]
    ssm_ch = ssm_d.shape[-1]
    assert conv_ch == ssm_ch and w_in.shape[-1] == 2 * conv_ch + 2 * ssm_ch
    n_groups, per_group = router_expert_w.shape[1], router_expert_w.shape[3]
    n_experts = n_groups * per_group
    assert n_groups + n_experts <= ROUTE_LANES
    rows = bsz * seq
    tile_e = min(EXPERT_TILE, rows // 8)

    mod = _ada(c, w_ada, b_ada)
    abar_re, abar_im, bb_re, bb_im = _ssm_prep(ssm_a_re, ssm_a_im, ssm_log_dt, ssm_b_re, ssm_b_im)
    xt = x

    for l in range(depth):
        shift1, scale1, gate1, shift2, scale2, gate2 = [mod[l, :, k * d:(k + 1) * d] for k in range(6)]

        p = _inproj(xt, norm1_g[l], scale1, shift1, w_in[l].astype(BF16))
        if l == 0:
            p, xt = p
        yc = _conv(p, conv_w[l], conv_b[l], conv_ln_g[l], conv_ln_b[l], out_norm_conv_g[l])
        bc, cc, ar, ai = _ssm_block_weights(abar_re[l], abar_im[l], bb_re[l], bb_im[l], ssm_c_re[l], ssm_c_im[l])
        ys = _ssm(p, 2 * conv_ch // ssm_ch, bc, cc, ar, ai, ssm_d[l], out_norm_ssm_g[l])

        wr = jnp.concatenate([router_group_w[l]] + [router_expert_w[l, g] for g in range(n_groups)], axis=-1)
        wr = jnp.pad(wr, ((0, 0), (0, ROUTE_LANES - wr.shape[1]))).astype(BF16)
        br = jnp.concatenate([router_group_b[l], router_expert_b[l].reshape(-1)])
        br = jnp.pad(br, (0, ROUTE_LANES - br.shape[0])).reshape(1, ROUTE_LANES)
        wo = w_out[l].astype(BF16).reshape(2, conv_ch, d)
        x1, h2, lp, lpt, tabs = _outproj(yc, ys, wo, xt, gate1, norm2_g[l], scale2, shift2, wr, br,
                                         n_groups, per_group)

        tab, tile_expert, n_valid, zrow, zflag, n_slots = _slots(tabs, rows, n_groups, n_experts, tile_e)
        hs = _dispatch(h2, lp, lpt, tab, zrow, zflag, n_valid, n_slots, tile_e)
        eo = _experts(hs, exp_w_gate, exp_w_up, exp_w_down, l, tile_expert, n_valid, tile_e)
        xt = _combine(eo, tab, lp, x1, gate2, final_norm_g, n_experts, final=l == depth - 1)

    return xt
```

```python
import functools

import jax
import jax.numpy as jnp
from jax import lax
from jax.experimental import pallas as pl
from jax.experimental.pallas import tpu as pltpu

EPS = 1e-6
F32 = jnp.float32
BF16 = jnp.bfloat16
I32 = jnp.int32

SUBLANES = 8
LANES = 128
MXU_DIM = 256
VMEM_LIMIT = 56 * 1024 * 1024
ROUTE_LANES = LANES
ROW_CHUNK = 16
CONV_ROWS = 32
CONV_LANES = 512
EXPERT_TILE = 512
ZERO_ROWS = 256


def _params(sem):
    return pltpu.CompilerParams(dimension_semantics=sem, vmem_limit_bytes=VMEM_LIMIT)


def _for_rows(n_rows, chunk, body, unroll=2):
    def step(i, carry):
        body(pl.multiple_of(i * chunk, chunk))
        return carry
    lax.fori_loop(0, n_rows // chunk, step, 0, unroll=unroll)


def _per_batch(v, rows):
    return jnp.tile(v, (rows // SUBLANES, 1))


def _ada_kernel(c_ref, w_ref, b_ref, o_ref):
    c = c_ref[...]
    ca = (c * jax.nn.sigmoid(c)).astype(BF16)
    o_ref[...] = jnp.dot(ca, w_ref[...].astype(BF16), preferred_element_type=F32) + b_ref[...]


def _ada(c, w_ada, b_ada):
    depth, d, n = w_ada.shape
    bsz = c.shape[0]
    tn = min(512, n)
    return pl.pallas_call(
        _ada_kernel,
        grid=(depth, n // tn),
        in_specs=[pl.BlockSpec((bsz, d), lambda l, j: (0, 0)),
                  pl.BlockSpec((None, d, tn), lambda l, j: (l, 0, j)),
                  pl.BlockSpec((None, 1, tn), lambda l, j: (l, 0, j))],
        out_specs=pl.BlockSpec((None, bsz, tn), lambda l, j: (l, 0, j)),
        out_shape=jax.ShapeDtypeStruct((depth, bsz, n), F32),
        compiler_params=_params(("arbitrary", "arbitrary")),
    )(c, w_ada, b_ada.reshape(depth, 1, n))


def _inproj_kernel(x_ref, g_ref, sc_ref, sh_ref, w_ref, *rest, tn, batch_major):
    if batch_major:
        o_ref, xt_ref, h_scr = rest
        xt_ref[...] = jnp.swapaxes(x_ref[...], 0, 1).reshape(xt_ref.shape)
        x_ref = xt_ref
    else:
        o_ref, h_scr = rest
    tm = x_ref.shape[0]
    g = g_ref[...]
    sc = _per_batch(1.0 + sc_ref[...], ROW_CHUNK)
    sh = _per_batch(sh_ref[...], ROW_CHUNK)

    def chunk(r0):
        x = x_ref[pl.ds(r0, ROW_CHUNK), :]
        ms = jnp.mean(x * x, axis=-1, keepdims=True)
        h = x * lax.rsqrt(ms + EPS) * g
        h_scr[pl.ds(r0, ROW_CHUNK), :] = (h * sc + sh).astype(BF16)
    _for_rows(tm, ROW_CHUNK, chunk, unroll=4)

    for n0 in range(0, w_ref.shape[1], tn):
        o_ref[:, n0:n0 + tn] = jnp.dot(
            h_scr[...], w_ref[:, n0:n0 + tn], preferred_element_type=F32).astype(o_ref.dtype)


def _inproj(x, g, scale, shift, w_bf16):
    batch_major = x.ndim == 3
    d = x.shape[-1]
    rows = x.size // d
    n = w_bf16.shape[1]
    tm = min(512, rows)
    tn = min(1024, n)
    row_spec = lambda w: pl.BlockSpec((tm, w), lambda i: (i, 0))
    if batch_major:
        x_spec = pl.BlockSpec((SUBLANES, tm // SUBLANES, d), lambda i: (0, i, 0))
        out_specs = [row_spec(n), row_spec(d)]
        out_shape = [jax.ShapeDtypeStruct((rows, n), BF16), jax.ShapeDtypeStruct((rows, d), F32)]
    else:
        x_spec = row_spec(d)
        out_specs = row_spec(n)
        out_shape = jax.ShapeDtypeStruct((rows, n), BF16)
    return pl.pallas_call(
        functools.partial(_inproj_kernel, tn=tn, batch_major=batch_major),
        grid=(rows // tm,),
        in_specs=[x_spec,
                  pl.BlockSpec((1, d), lambda i: (0, 0)),
                  pl.BlockSpec((SUBLANES, d), lambda i: (0, 0)),
                  pl.BlockSpec((SUBLANES, d), lambda i: (0, 0)),
                  pl.BlockSpec((d, n), lambda i: (0, 0))],
        out_specs=out_specs,
        out_shape=out_shape,
        scratch_shapes=[pltpu.VMEM((tm, d), BF16)],
        compiler_params=_params(("arbitrary",)),
    )(x, g.reshape(1, d), scale, shift, w_bf16)


def _conv_kernel(v_ref, gt_ref, w_ref, cb_ref, lg_ref, lb_ref, og_ref, o_ref, z_scr, a_scr, *, width):
    rows, ch = v_ref.shape
    halo = (width - 1) * SUBLANES

    @pl.when(pl.program_id(0) == 0)
    def _():
        z_scr[pl.ds(0, halo), :] = jnp.zeros((halo, ch), F32)

    def glu(r0):
        v = v_ref[pl.ds(r0, ROW_CHUNK), :].astype(F32)
        gt = gt_ref[pl.ds(r0, ROW_CHUNK), :].astype(F32)
        z_scr[pl.ds(halo + r0, ROW_CHUNK), :] = v * jax.nn.sigmoid(gt)
    _for_rows(rows, ROW_CHUNK, glu)

    lg = lg_ref[...]
    lb = lb_ref[...]
    og = og_ref[...]
    reps = CONV_ROWS // SUBLANES
    lanes = min(CONV_LANES, ch)

    def taps(r0):
        for c0 in range(0, ch, lanes):
            cb = cb_ref[:, c0:c0 + lanes]
            acc = [cb] * reps
            w = {}
            for j in range(reps + width - 1):
                z = z_scr[pl.ds(r0 + j * SUBLANES, SUBLANES), c0:c0 + lanes]
                if j < width:
                    w[j] = w_ref[pl.ds(j * SUBLANES, SUBLANES), c0:c0 + lanes]
                for m in range(reps):
                    k = j - m
                    if 0 <= k < width:
                        acc[m] = acc[m] + w[k] * z
            for m in range(reps):
                a_scr[pl.ds(r0 + m * SUBLANES, SUBLANES), c0:c0 + lanes] = acc[m]
    _for_rows(rows, CONV_ROWS, taps, unroll=1)

    def conv(r0):
        acc = a_scr[pl.ds(r0, ROW_CHUNK), :]
        mu = jnp.mean(acc, axis=-1, keepdims=True)
        cen = acc - mu
        var = jnp.mean(cen * cen, axis=-1, keepdims=True)
        y = cen * lax.rsqrt(var + EPS) * lg + lb
        s = y * jax.nn.sigmoid(y)
        ms = jnp.mean(s * s, axis=-1, keepdims=True)
        o_ref[pl.ds(r0, ROW_CHUNK), :] = (s * lax.rsqrt(ms + EPS) * og).astype(BF16)
    _for_rows(rows, ROW_CHUNK, conv, unroll=8)

    z_scr[pl.ds(0, halo), :] = z_scr[pl.ds(rows, halo), :]


def _conv(p, conv_w, conv_b, ln_g, ln_b, out_g):
    rows = p.shape[0]
    width, ch = conv_w.shape
    halo = (width - 1) * SUBLANES
    tr = min(512, rows)
    assert tr >= halo
    vec = lambda a: a.reshape(1, ch)
    row_spec = lambda col: pl.BlockSpec((tr, ch), lambda i, col=col: (i, col))
    one = pl.BlockSpec((1, ch), lambda i: (0, 0))
    w8 = jnp.broadcast_to(conv_w[:, None, :], (width, SUBLANES, ch)).reshape(width * SUBLANES, ch)
    cb8 = jnp.broadcast_to(conv_b[None, :], (SUBLANES, ch))
    return pl.pallas_call(
        functools.partial(_conv_kernel, width=width),
        grid=(rows // tr,),
        in_specs=[row_spec(0), row_spec(1),
                  pl.BlockSpec((width * SUBLANES, ch), lambda i: (0, 0)),
                  pl.BlockSpec((SUBLANES, ch), lambda i: (0, 0)), one, one, one],
        out_specs=pl.BlockSpec((tr, ch), lambda i: (i, 0)),
        out_shape=jax.ShapeDtypeStruct((rows, ch), BF16),
        scratch_shapes=[pltpu.VMEM((halo + tr, ch), F32), pltpu.VMEM((tr, ch), F32)],
        compiler_params=_params(("arbitrary",)),
    )(p, p, w8, cb8, vec(ln_g), vec(ln_b), vec(out_g))


def _ssm_prep_kernel(lr_ref, li_ref, ldt_ref, br_ref, bi_ref, ar_o, ai_o, bbr_o, bbi_o):
    lr = lr_ref[...]
    li = li_ref[...]
    dt = jnp.exp(ldt_ref[...])
    mag = jnp.exp(lr * dt)
    ang = li * dt
    ar = mag * jnp.cos(ang)
    ai = mag * jnp.sin(ang)
    den = lr * lr + li * li
    nr = ar - 1.0
    ni = ai
    fr = (nr * lr + ni * li) / den
    fi = (ni * lr - nr * li) / den
    br = br_ref[...]
    bi = bi_ref[...]
    ar_o[...] = ar
    ai_o[...] = ai
    bbr_o[...] = fr * br - fi * bi
    bbi_o[...] = fr * bi + fi * br


def _ssm_prep(a_re, a_im, log_dt, b_re, b_im):
    depth, g, p = a_re.shape
    h = b_re.shape[-1]
    rep = lambda a: jnp.broadcast_to(a[:, :, None, :], (depth, g, h, p)).reshape(depth * g * h, p)
    ldt = jnp.broadcast_to(log_dt[:, :, None, None], (depth, g, h, p)).reshape(depth * g * h, p)
    bt = lambda b: jnp.swapaxes(b, -1, -2).reshape(depth * g * h, p)
    n = g * h
    spec = pl.BlockSpec((n, p), lambda l: (l, 0))
    shp = jax.ShapeDtypeStruct((depth * n, p), F32)
    ar, ai, bbr, bbi = pl.pallas_call(
        _ssm_prep_kernel,
        grid=(depth,),
        in_specs=[spec] * 5,
        out_specs=[spec] * 4,
        out_shape=[shp] * 4,
        compiler_params=_params(("arbitrary",)),
    )(rep(a_re), rep(a_im), ldt, bt(b_re), bt(b_im))
    r4 = lambda a: a.reshape(depth, g, h, p)
    return r4(ar)[:, :, 0, :], r4(ai)[:, :, 0, :], r4(bbr), r4(bbi)


def _ssm_block_weights(abar_re, abar_im, bb_re, bb_im, c_re, c_im):
    g, h, p = bb_re.shape
    gpt = min(MXU_DIM // h, g)
    kt = g // gpt
    eye = jnp.eye(gpt, dtype=F32)[None, :, None, :, None]

    def bmat(bb):
        b5 = bb.reshape(kt, gpt, h, 1, p)
        return (b5 * eye).astype(BF16).reshape(kt, gpt * h, gpt * p)

    def cmat(c):
        c5 = jnp.swapaxes(c.reshape(kt, gpt, h, p), 2, 3).reshape(kt, gpt, p, 1, h)
        return (c5 * eye).astype(BF16).reshape(kt, gpt * p, gpt * h)

    bc = jnp.concatenate([bmat(bb_re), bmat(bb_im)], axis=-1)
    cc = jnp.concatenate([cmat(c_re), cmat(-c_im)], axis=1)
    bcast = lambda a: jnp.broadcast_to(a.reshape(1, g * p), (SUBLANES, g * p))
    return bc, cc, bcast(abar_re), bcast(abar_im)


def _ssm_kernel(u_ref, gt_ref, bc_ref, cc_ref, ar_ref, ai_ref, d_ref, og_ref, o_ref,
                s_scr, st_scr, y_scr, *, lane_chunk):
    rows, ch = u_ref.shape
    kt, kch, sw2 = bc_ref.shape
    sw = sw2 // 2
    steps = rows // SUBLANES

    @pl.when(pl.program_id(0) == 0)
    def _():
        st_scr[...] = jnp.zeros(st_scr.shape, F32)

    for k in range(kt):
        ub = u_ref[:, k * kch:(k + 1) * kch].astype(BF16)
        s_scr[:, k * sw2:(k + 1) * sw2] = jnp.dot(ub, bc_ref[k], preferred_element_type=F32)

    for k in range(kt):
        for c in range(sw // lane_chunk):
            re0 = k * sw2 + c * lane_chunk
            im0 = re0 + sw
            a0 = k * sw + c * lane_chunk
            ar = ar_ref[:, a0:a0 + lane_chunk]
            ai = ai_ref[:, a0:a0 + lane_chunk]

            def step(t, carry, re0=re0, im0=im0, ar=ar, ai=ai):
                sr, si = carry
                r = pl.multiple_of(t * SUBLANES, SUBLANES)
                br = s_scr[pl.ds(r, SUBLANES), re0:re0 + lane_chunk]
                bi = s_scr[pl.ds(r, SUBLANES), im0:im0 + lane_chunk]
                nr = ar * sr - ai * si + br
                ni = ar * si + ai * sr + bi
                s_scr[pl.ds(r, SUBLANES), re0:re0 + lane_chunk] = nr
                s_scr[pl.ds(r, SUBLANES), im0:im0 + lane_chunk] = ni
                return nr, ni

            sr, si = lax.fori_loop(
                0, steps, step,
                (st_scr[:, re0:re0 + lane_chunk], st_scr[:, im0:im0 + lane_chunk]))
            st_scr[:, re0:re0 + lane_chunk] = sr
            st_scr[:, im0:im0 + lane_chunk] = si

    for k in range(kt):
        sb = s_scr[:, k * sw2:(k + 1) * sw2].astype(BF16)
        y_scr[:, k * kch:(k + 1) * kch] = jnp.dot(sb, cc_ref[k], preferred_element_type=F32)

    d = d_ref[...]
    og = og_ref[...]

    def gate(r0):
        u = u_ref[pl.ds(r0, ROW_CHUNK), :].astype(F32)
        y = y_scr[pl.ds(r0, ROW_CHUNK), :] + u * d
        v = jax.nn.gelu(y) * jax.nn.sigmoid(gt_ref[pl.ds(r0, ROW_CHUNK), :].astype(F32))
        ms = jnp.mean(v * v, axis=-1, keepdims=True)
        o_ref[pl.ds(r0, ROW_CHUNK), :] = (v * lax.rsqrt(ms + EPS) * og).astype(BF16)
    _for_rows(rows, ROW_CHUNK, gate, unroll=4)


def _ssm(p, col0, bc, cc, ar, ai, d_skip, out_g):
    rows = p.shape[0]
    kt, kch, sw2 = bc.shape
    ch = kt * kch
    tr = min(256, rows)
    lane_chunk = min(512, sw2 // 2)
    vec = lambda a: a.reshape(1, ch)
    row_spec = lambda col: pl.BlockSpec((tr, ch), lambda i, col=col: (i, col))
    one = pl.BlockSpec((1, ch), lambda i: (0, 0))
    full = lambda a: pl.BlockSpec(a.shape, lambda i, nd=a.ndim: (0,) * nd)
    return pl.pallas_call(
        functools.partial(_ssm_kernel, lane_chunk=lane_chunk),
        grid=(rows // tr,),
        in_specs=[row_spec(col0), row_spec(col0 + 1), full(bc), full(cc), full(ar), full(ai), one, one],
        out_specs=pl.BlockSpec((tr, ch), lambda i: (i, 0)),
        out_shape=jax.ShapeDtypeStruct((rows, ch), BF16),
        scratch_shapes=[pltpu.VMEM((tr, kt * sw2), F32),
                        pltpu.VMEM((SUBLANES, kt * sw2), F32),
                        pltpu.VMEM((tr, ch), F32)],
        compiler_params=_params(("arbitrary",)),
    )(p, p, bc, cc, ar, ai, vec(d_skip), vec(out_g))


def _outproj_kernel(yc_ref, ys_ref, wo_ref, x_ref, g1_ref, n2_ref, sc_ref, sh_ref, wr_ref, br_ref,
                    x1_ref, h2_ref, lp_ref, lpt_ref, tab_ref, *, n_groups, per_group):
    tm = x_ref.shape[0]

    x1_ref[...] = (jnp.dot(yc_ref[...], wo_ref[0], preferred_element_type=F32)
                   + jnp.dot(ys_ref[...], wo_ref[1], preferred_element_type=F32))

    g1 = _per_batch(g1_ref[...], ROW_CHUNK)
    sc = _per_batch(1.0 + sc_ref[...], ROW_CHUNK)
    sh = _per_batch(sh_ref[...], ROW_CHUNK)
    n2 = n2_ref[...]

    def chunk(r0):
        x1 = x_ref[pl.ds(r0, ROW_CHUNK), :] + g1 * x1_ref[pl.ds(r0, ROW_CHUNK), :]
        x1_ref[pl.ds(r0, ROW_CHUNK), :] = x1
        ms = jnp.mean(x1 * x1, axis=-1, keepdims=True)
        h2 = x1 * lax.rsqrt(ms + EPS) * n2 * sc + sh
        h2_ref[pl.ds(r0, ROW_CHUNK), :] = h2.astype(BF16)
    _for_rows(tm, ROW_CHUNK, chunk, unroll=4)

    lg = jnp.dot(h2_ref[...], wr_ref[...], preferred_element_type=F32) + br_ref[...]
    lane = lax.broadcasted_iota(I32, lg.shape, 1)
    neg = jnp.float32(-jnp.inf)
    big = jnp.int32(ROUTE_LANES)

    def first_max(vals):
        m = jnp.max(vals, axis=-1, keepdims=True)
        idx = jnp.min(jnp.where(vals == m, lane, big), axis=-1, keepdims=True)
        return m, idx

    gl = jnp.where(lane < n_groups, lg, neg)
    gmax, gidx = first_max(gl)
    g_w = 1.0 / jnp.sum(jnp.exp(gl - gmax), axis=-1, keepdims=True)
    lo = n_groups + gidx * per_group
    el = jnp.where((lane >= lo) & (lane < lo + per_group), lg, neg)
    v1, i1 = first_max(el)
    v2, i2 = first_max(jnp.where(lane == i1, neg, el))
    e21 = jnp.exp(v2 - v1)
    w1 = g_w / (1.0 + e21)
    w2 = g_w * e21 / (1.0 + e21)

    hit1 = lane == i1
    hit2 = lane == i2
    onehot = jnp.where(hit1 | hit2, 1.0, 0.0)
    row = lax.broadcasted_iota(I32, (tm, tm), 0)
    col = lax.broadcasted_iota(I32, (tm, tm), 1)
    before = jnp.where(col < row, 1.0, 0.0).astype(BF16)
    earlier = jnp.dot(before, onehot.astype(BF16), preferred_element_type=F32)
    count = jnp.sum(onehot, axis=0, keepdims=True).astype(I32)
    blocks = (count + (SUBLANES - 1)) // SUBLANES
    lrow = lax.broadcasted_iota(I32, (ROUTE_LANES, ROUTE_LANES), 0)
    lcol = lax.broadcasted_iota(I32, (ROUTE_LANES, ROUTE_LANES), 1)
    below = jnp.where(lrow < lcol, 1.0, 0.0).astype(BF16)
    blocks8 = jnp.broadcast_to(blocks.astype(F32), (SUBLANES, ROUTE_LANES)).astype(BF16)
    first = jnp.dot(blocks8, below, preferred_element_type=F32)[0:1] * SUBLANES
    slot = first + earlier
    lpos1 = jnp.sum(jnp.where(hit1, slot, 0.0), axis=-1, keepdims=True)
    lpos2 = jnp.sum(jnp.where(hit2, slot, 0.0), axis=-1, keepdims=True)

    rec = jnp.where(lane == 0, lpos1, jnp.where(lane == 1, lpos2, jnp.where(
        lane == 2, w1, jnp.where(lane == 3, w2, 0.0))))
    lp_ref[...] = rec
    lpt_ref[0] = jnp.transpose(rec)[0:SUBLANES, :]
    sub = lax.broadcasted_iota(I32, (SUBLANES, ROUTE_LANES), 0)
    tab_ref[0] = jnp.where(sub == 0, first.astype(I32), jnp.where(sub == 1, blocks, 0))


def _outproj(yc, ys, wo_bf16, xt, gate1, n2_g, scale2, shift2, wr_bf16, br, n_groups, per_group):
    rows, d = xt.shape
    ch = yc.shape[1]
    tm = min(256, rows)
    row_spec = lambda w: pl.BlockSpec((tm, w), lambda i: (i, 0))
    one = pl.BlockSpec((1, d), lambda i: (0, 0))
    per_b = pl.BlockSpec((SUBLANES, d), lambda i: (0, 0))
    kern = functools.partial(_outproj_kernel, n_groups=n_groups, per_group=per_group)
    return pl.pallas_call(
        kern,
        grid=(rows // tm,),
        in_specs=[row_spec(ch), row_spec(ch),
                  pl.BlockSpec((2, ch, d), lambda i: (0, 0, 0)),
                  row_spec(d), per_b, one, per_b, per_b,
                  pl.BlockSpec((d, ROUTE_LANES), lambda i: (0, 0)),
                  pl.BlockSpec((1, ROUTE_LANES), lambda i: (0, 0))],
        out_specs=[row_spec(d), row_spec(d), row_spec(ROUTE_LANES),
                   pl.BlockSpec((1, SUBLANES, tm), lambda i: (i, 0, 0)),
                   pl.BlockSpec((1, SUBLANES, ROUTE_LANES), lambda i: (i, 0, 0))],
        out_shape=[jax.ShapeDtypeStruct((rows, d), F32),
                   jax.ShapeDtypeStruct((rows, d), BF16),
                   jax.ShapeDtypeStruct((rows, ROUTE_LANES), F32),
                   jax.ShapeDtypeStruct((rows // tm, SUBLANES, tm), F32),
                   jax.ShapeDtypeStruct((rows // tm, SUBLANES, ROUTE_LANES), I32)],
        compiler_params=_params(("arbitrary",)),
    )(yc, ys, wo_bf16, xt, gate1, n2_g.reshape(1, d), scale2, shift2, wr_bf16, br)


def _for_blocks(tab_ref, tile, n_experts, fn):
    for e in range(n_experts):
        lrow0 = tab_ref[tile, e]
        grow0 = tab_ref[tile, 2 * n_experts + e]

        def block(c, carry, lrow0=lrow0, grow0=grow0):
            off = c * SUBLANES
            fn(pl.multiple_of(lrow0 + off, SUBLANES), pl.multiple_of(grow0 + off, SUBLANES))
            return carry
        lax.fori_loop(0, tab_ref[tile, n_experts + e], block, 0)


def _dispatch_kernel(zrow_ref, zcnt_ref, tab_ref, lpt_ref, lp_ref, h_ref, hs_ref,
                     sorted_scr, zero_scr, sem, zsem, *, n_experts):
    tm, d = h_ref.shape
    n_sorted = sorted_scr.shape[1]
    i = pl.program_id(0)
    slot = i % 2

    def each_clear(fn):
        big = zero_scr.shape[0]
        per_big = big // SUBLANES
        for e in range(n_experts + 1):
            row0 = zrow_ref[e]
            n_big = zcnt_ref[e] // per_big
            n_small = zcnt_ref[e] - n_big * per_big
            rest0 = row0 + n_big * big

            def big_block(c, carry, row0=row0):
                fn(pltpu.make_async_copy(
                    zero_scr, hs_ref.at[pl.ds(pl.multiple_of(row0 + c * big, SUBLANES), big)], zsem))
                return carry
            lax.fori_loop(0, n_big, big_block, 0)

            def small_block(c, carry, rest0=rest0):
                fn(pltpu.make_async_copy(
                    zero_scr.at[pl.ds(0, SUBLANES)],
                    hs_ref.at[pl.ds(pl.multiple_of(rest0 + c * SUBLANES, SUBLANES), SUBLANES)], zsem))
                return carry
            lax.fori_loop(0, n_small, small_block, 0)

    @pl.when(i == 0)
    def _():
        zero_scr[...] = jnp.zeros(zero_scr.shape, F32)
        each_clear(lambda cp: cp.start())

    srow = lax.broadcasted_iota(I32, (n_sorted, tm), 0)
    p1 = jnp.where(srow == lpt_ref[0, 0:1, :].astype(I32), 1.0, 0.0)
    p2 = jnp.where(srow == lpt_ref[0, 1:2, :].astype(I32), 1.0, 0.0)
    sorted_scr[slot, :, 0:d] = jnp.dot((p1 + p2).astype(BF16), h_ref[...], preferred_element_type=F32)

    lane = lax.broadcasted_iota(I32, (tm, ROUTE_LANES), 1)

    def pieces(w):
        hi = w.astype(BF16).astype(F32)
        mid = (w - hi).astype(BF16).astype(F32)
        lo = w - hi - mid
        return jnp.where(lane == 0, hi, jnp.where(lane == 1, mid, jnp.where(lane == 2, lo, 0.0))).astype(BF16)

    lp = lp_ref[...]
    sorted_scr[slot, :, d:d + ROUTE_LANES] = (
        jnp.dot(p1.astype(BF16), pieces(lp[:, 2:3]), preferred_element_type=F32)
        + jnp.dot(p2.astype(BF16), pieces(lp[:, 3:4]), preferred_element_type=F32))

    def copy(buf):
        return lambda lrow, grow: pltpu.make_async_copy(
            sorted_scr.at[buf, pl.ds(lrow, SUBLANES)], hs_ref.at[pl.ds(grow, SUBLANES)], sem.at[buf])

    _for_blocks(tab_ref, i, n_experts, lambda lrow, grow: copy(slot)(lrow, grow).start())

    @pl.when(i > 0)
    def _():
        _for_blocks(tab_ref, i - 1, n_experts, lambda lrow, grow: copy(1 - slot)(lrow, grow).wait())

    @pl.when(i == pl.num_programs(0) - 1)
    def _():
        _for_blocks(tab_ref, i, n_experts, lambda lrow, grow: copy(slot)(lrow, grow).wait())
        each_clear(lambda cp: cp.wait())


def _dispatch(h2, lp, lpt, tab, zrow, zcnt, n_slots):
    rows, d = h2.shape
    tm = lpt.shape[-1]
    n_experts = zrow.shape[0] - 1
    n_sorted = 2 * tm + n_experts * (SUBLANES - 1)
    n_sorted += -n_sorted % SUBLANES
    width = d + ROUTE_LANES
    grid_spec = pltpu.PrefetchScalarGridSpec(
        num_scalar_prefetch=3,
        grid=(rows // tm,),
        in_specs=[pl.BlockSpec((1, SUBLANES, tm), lambda i, *_: (i, 0, 0)),
                  pl.BlockSpec((tm, ROUTE_LANES), lambda i, *_: (i, 0)),
                  pl.BlockSpec((tm, d), lambda i, *_: (i, 0))],
        out_specs=pl.BlockSpec(memory_space=pl.ANY),
        scratch_shapes=[pltpu.VMEM((2, n_sorted, width), F32), pltpu.VMEM((ZERO_ROWS, width), F32),
                        pltpu.SemaphoreType.DMA((2,)), pltpu.SemaphoreType.DMA(())],
    )
    return pl.pallas_call(
        functools.partial(_dispatch_kernel, n_experts=n_experts),
        grid_spec=grid_spec,
        out_shape=jax.ShapeDtypeStruct((n_slots, width), F32),
        compiler_params=_params(("arbitrary",)),
    )(zrow, zcnt, tab, lpt, lp, h2)


def _expert_kernel(te_ref, nv_ref, hs_ref, wg_ref, wu_ref, wd_ref, o_ref, wg_scr, wu_scr, wd_scr):
    i = pl.program_id(0)

    @pl.when(i < nv_ref[0])
    def _():
        @pl.when((i == 0) | (te_ref[i] != te_ref[jnp.maximum(i - 1, 0)]))
        def _():
            wg_scr[...] = wg_ref[...].astype(BF16)
            wu_scr[...] = wu_ref[...].astype(BF16)
            wd_scr[...] = wd_ref[...].astype(BF16)

        d = wg_scr.shape[0]
        hb = hs_ref[:, 0:d].astype(BF16)
        w = hs_ref[:, d:d + 1] + hs_ref[:, d + 1:d + 2] + hs_ref[:, d + 2:d + 3]
        a = jnp.dot(hb, wg_scr[...], preferred_element_type=F32)
        u = jnp.dot(hb, wu_scr[...], preferred_element_type=F32)
        hm = (a * jax.nn.sigmoid(a) * u * w).astype(BF16)
        o_ref[...] = jnp.dot(hm, wd_scr[...], preferred_element_type=F32)

    @pl.when(i >= nv_ref[0])
    def _():
        o_ref[...] = jnp.zeros(o_ref.shape, F32)


def _experts(hs, w_gate, w_up, w_down, layer, tile_expert, n_valid, tile_e):
    n_slots, width = hs.shape
    d = width - ROUTE_LANES
    per_group = w_gate.shape[2]
    f = w_gate.shape[-1]
    n_tiles = n_slots // tile_e

    def row_map(i, te, nv):
        return (jnp.minimum(i, nv[0] - 1), 0)

    def w_map(i, te, nv):
        e = te[i]
        return (layer, e // per_group, e % per_group, 0, 0)

    grid_spec = pltpu.PrefetchScalarGridSpec(
        num_scalar_prefetch=2,
        grid=(n_tiles,),
        in_specs=[pl.BlockSpec((tile_e, width), row_map),
                  pl.BlockSpec((None, None, None, d, f), w_map),
                  pl.BlockSpec((None, None, None, d, f), w_map),
                  pl.BlockSpec((None, None, None, f, d), w_map)],
        out_specs=pl.BlockSpec((tile_e, d), lambda i, te, nv: (i, 0)),
        scratch_shapes=[pltpu.VMEM((d, f), BF16), pltpu.VMEM((d, f), BF16), pltpu.VMEM((f, d), BF16)],
    )
    return pl.pallas_call(
        _expert_kernel,
        grid_spec=grid_spec,
        out_shape=jax.ShapeDtypeStruct((n_slots, d), F32),
        compiler_params=_params(("arbitrary",)),
    )(tile_expert, n_valid, hs, w_gate, w_up, w_down)


def _combine_kernel(tab_ref, eo_ref, lp_ref, x_ref, g2_ref, gf_ref, o_ref, sorted_scr, y_scr, sem,
                    *, n_experts, final):
    tm, d = x_ref.shape
    n_sorted = sorted_scr.shape[1]
    i = pl.program_id(0)
    slot = i % 2

    def fetch(buf):
        return lambda lrow, grow: pltpu.make_async_copy(
            eo_ref.at[pl.ds(grow, SUBLANES)], sorted_scr.at[buf, pl.ds(lrow, SUBLANES)], sem.at[buf])

    @pl.when(i == 0)
    def _():
        for buf in range(2):
            sorted_scr[buf, pl.ds(2 * tm, n_sorted - 2 * tm), :] = jnp.zeros((n_sorted - 2 * tm, d), F32)
        _for_blocks(tab_ref, 0, n_experts, lambda lrow, grow: fetch(0)(lrow, grow).start())

    @pl.when(i + 1 < pl.num_programs(0))
    def _():
        _for_blocks(tab_ref, i + 1, n_experts, lambda lrow, grow: fetch(1 - slot)(lrow, grow).start())

    _for_blocks(tab_ref, i, n_experts, lambda lrow, grow: fetch(slot)(lrow, grow).wait())

    lp = lp_ref[...]
    scol = lax.broadcasted_iota(I32, (tm, n_sorted), 1)
    q = (jnp.where(scol == lp[:, 0:1].astype(I32), 1.0, 0.0)
         + jnp.where(scol == lp[:, 1:2].astype(I32), 1.0, 0.0)).astype(BF16)
    y_scr[...] = jnp.dot(q, sorted_scr[slot].astype(BF16), preferred_element_type=F32)

    g2 = _per_batch(g2_ref[...], ROW_CHUNK)
    gf = gf_ref[...]

    def chunk(r0):
        x2 = x_ref[pl.ds(r0, ROW_CHUNK), :] + g2 * y_scr[pl.ds(r0, ROW_CHUNK), :]
        if final:
            ms = jnp.mean(x2 * x2, axis=-1, keepdims=True)
            y_scr[pl.ds(r0, ROW_CHUNK), :] = x2 * lax.rsqrt(ms + EPS) * gf
        else:
            o_ref[pl.ds(r0, ROW_CHUNK), :] = x2
    _for_rows(tm, ROW_CHUNK, chunk, unroll=4)

    if final:
        o_ref[...] = jnp.swapaxes(y_scr[...].reshape(tm // SUBLANES, SUBLANES, d), 0, 1)


def _combine(eo, tab, lp, x1, gate2, final_g, n_experts, final):
    rows, d = x1.shape
    tm = rows // tab.shape[0]
    n_sorted = 2 * tm + n_experts * (SUBLANES - 1)
    n_sorted += -n_sorted % SUBLANES
    if final:
        out_spec = pl.BlockSpec((SUBLANES, tm // SUBLANES, d), lambda i, *_: (0, i, 0))
        out_shape = jax.ShapeDtypeStruct((SUBLANES, rows // SUBLANES, d), F32)
    else:
        out_spec = pl.BlockSpec((tm, d), lambda i, *_: (i, 0))
        out_shape = jax.ShapeDtypeStruct((rows, d), F32)
    grid_spec = pltpu.PrefetchScalarGridSpec(
        num_scalar_prefetch=1,
        grid=(rows // tm,),
        in_specs=[pl.BlockSpec(memory_space=pl.ANY),
                  pl.BlockSpec((tm, ROUTE_LANES), lambda i, *_: (i, 0)),
                  pl.BlockSpec((tm, d), lambda i, *_: (i, 0)),
                  pl.BlockSpec((SUBLANES, d), lambda i, *_: (0, 0)),
                  pl.BlockSpec((1, d), lambda i, *_: (0, 0))],
        out_specs=out_spec,
        scratch_shapes=[pltpu.VMEM((2, n_sorted, d), F32), pltpu.VMEM((tm, d), F32),
                        pltpu.SemaphoreType.DMA((2,))],
    )
    return pl.pallas_call(
        functools.partial(_combine_kernel, n_experts=n_experts, final=final),
        grid_spec=grid_spec,
        out_shape=out_shape,
        compiler_params=_params(("arbitrary",)),
    )(tab, eo, lp, x1, gate2, final_g.reshape(1, d))


def _slots(tabs, rows, n_groups, n_experts, tile_e):
    n_tok_tiles = tabs.shape[0]
    first = tabs[:, 0, n_groups:n_groups + n_experts]
    blocks = tabs[:, 1, n_groups:n_groups + n_experts]
    run = blocks * SUBLANES
    total = jnp.sum(run, axis=0)
    region = ((total + tile_e - 1) // tile_e) * tile_e
    ends = jnp.cumsum(region)
    starts = ends - region
    gfirst = starts[None, :] + jnp.cumsum(run, axis=0) - run
    tab = jnp.concatenate([first, blocks, gfirst], axis=1).astype(I32)
    max_rows = 2 * rows + n_tok_tiles * n_experts * (SUBLANES - 1) + n_experts * (tile_e - SUBLANES)
    n_tiles = max_rows // tile_e
    n_valid = ends[-1] // tile_e
    tile_start = jnp.minimum(jnp.arange(n_tiles, dtype=I32), jnp.maximum(n_valid - 1, 0)) * tile_e
    tile_expert = jnp.sum((ends[None, :] <= tile_start[:, None]).astype(I32), axis=1)
    tile_expert = jnp.minimum(tile_expert, n_experts - 1).astype(I32)
    n_slots = n_tiles * tile_e
    zrow = jnp.concatenate([starts + total, ends[-1:]]).astype(I32)
    zcnt = (jnp.concatenate([region - total, n_slots - ends[-1:]]) // SUBLANES).astype(I32)
    return tab, tile_expert, n_valid.reshape(1).astype(I32), zrow, zcnt, n_slots


def kernel(x, c, w_ada, b_ada, norm1_g, w_in, conv_w, conv_b, conv_ln_g, conv_ln_b, ssm_a_re, ssm_a_im, ssm_b_re, ssm_b_im, ssm_c_re, ssm_c_im, ssm_d, ssm_log_dt, out_norm_conv_g, out_norm_ssm_g, w_out, norm2_g, router_group_w, router_group_b, router_expert_w, router_expert_b, exp_w_gate, exp_w_up, exp_w_down, final_norm_g):
    bsz, seq, d = x.shape
    assert bsz == SUBLANES
    depth = w_ada.shape[0]
    conv_ch = conv_w.shape[-1]
    ssm_ch = ssm_d.shape[-1]
    assert conv_ch == ssm_ch and w_in.shape[-1] == 2 * conv_ch + 2 * ssm_ch
    n_groups, per_group = router_expert_w.shape[1], router_expert_w.shape[3]
    n_experts = n_groups * per_group
    assert n_groups + n_experts <= ROUTE_LANES
    rows = bsz * seq
    tile_e = min(EXPERT_TILE, rows // 8)

    mod = _ada(c, w_ada, b_ada)
    abar_re, abar_im, bb_re, bb_im = _ssm_prep(ssm_a_re, ssm_a_im, ssm_log_dt, ssm_b_re, ssm_b_im)
    xt = x

    for l in range(depth):
        shift1, scale1, gate1, shift2, scale2, gate2 = [mod[l, :, k * d:(k + 1) * d] for k in range(6)]

        p = _inproj(xt, norm1_g[l], scale1, shift1, w_in[l].astype(BF16))
        if l == 0:
            p, xt = p
        yc = _conv(p, conv_w[l], conv_b[l], conv_ln_g[l], conv_ln_b[l], out_norm_conv_g[l])
        bc, cc, ar, ai = _ssm_block_weights(abar_re[l], abar_im[l], bb_re[l], bb_im[l], ssm_c_re[l], ssm_c_im[l])
        ys = _ssm(p, 2 * conv_ch // ssm_ch, bc, cc, ar, ai, ssm_d[l], out_norm_ssm_g[l])

        wr = jnp.concatenate([router_group_w[l]] + [router_expert_w[l, g] for g in range(n_groups)], axis=-1)
        wr = jnp.pad(wr, ((0, 0), (0, ROUTE_LANES - wr.shape[1]))).astype(BF16)
        br = jnp.concatenate([router_group_b[l], router_expert_b[l].reshape(-1)])
        br = jnp.pad(br, (0, ROUTE_LANES - br.shape[0])).reshape(1, ROUTE_LANES)
        wo = w_out[l].astype(BF16).reshape(2, conv_ch, d)
        x1, h2, lp, lpt, tabs = _outproj(yc, ys, wo, xt, gate1, norm2_g[l], scale2, shift2, wr, br,
                                         n_groups, per_group)

        tab, tile_expert, n_valid, zrow, zcnt, n_slots = _slots(tabs, rows, n_groups, n_experts, tile_e)
        hs = _dispatch(h2, lp, lpt, tab, zrow, zcnt, n_slots)
        eo = _experts(hs, exp_w_gate, exp_w_up, exp_w_down, l, tile_expert, n_valid, tile_e)
        xt = _combine(eo, tab, lp, x1, gate2, final_norm_g, n_experts, final=l == depth - 1)

    return xt
```

```python
import functools

import jax
import jax.numpy as jnp
from jax import lax
from jax.experimental import pallas as pl
from jax.experimental.pallas import tpu as pltpu

EPS = 1e-6
F32 = jnp.float32
BF16 = jnp.bfloat16
I32 = jnp.int32

SUBLANES = 8
LANES = 128
MXU_DIM = 256
VMEM_LIMIT = 56 * 1024 * 1024
ROUTE_LANES = LANES
ROW_CHUNK = 16
CONV_ROWS = 32
CONV_LANES = 512
EXPERT_TILE = 512
ZERO_ROWS = 256


def _params(sem):
    return pltpu.CompilerParams(dimension_semantics=sem, vmem_limit_bytes=VMEM_LIMIT)


def _for_rows(n_rows, chunk, body, unroll=2):
    def step(i, carry):
        body(pl.multiple_of(i * chunk, chunk))
        return carry
    lax.fori_loop(0, n_rows // chunk, step, 0, unroll=unroll)


def _per_batch(v, rows):
    return jnp.tile(v, (rows // SUBLANES, 1))


def _ada_kernel(c_ref, w_ref, b_ref, o_ref):
    c = c_ref[...]
    ca = (c * jax.nn.sigmoid(c)).astype(BF16)
    o_ref[...] = jnp.dot(ca, w_ref[...].astype(BF16), preferred_element_type=F32) + b_ref[...]


def _ada(c, w_ada, b_ada):
    depth, d, n = w_ada.shape
    bsz = c.shape[0]
    tn = min(512, n)
    return pl.pallas_call(
        _ada_kernel,
        grid=(depth, n // tn),
        in_specs=[pl.BlockSpec((bsz, d), lambda l, j: (0, 0)),
                  pl.BlockSpec((None, d, tn), lambda l, j: (l, 0, j)),
                  pl.BlockSpec((None, 1, tn), lambda l, j: (l, 0, j))],
        out_specs=pl.BlockSpec((None, bsz, tn), lambda l, j: (l, 0, j)),
        out_shape=jax.ShapeDtypeStruct((depth, bsz, n), F32),
        compiler_params=_params(("arbitrary", "arbitrary")),
    )(c, w_ada, b_ada.reshape(depth, 1, n))


def _inproj_kernel(x_ref, g_ref, sc_ref, sh_ref, w_ref, *rest, tn, batch_major):
    if batch_major:
        o_ref, xt_ref, h_scr = rest
        xt_ref[...] = jnp.swapaxes(x_ref[...], 0, 1).reshape(xt_ref.shape)
        x_ref = xt_ref
    else:
        o_ref, h_scr = rest
    tm = x_ref.shape[0]
    g = g_ref[...]
    sc = _per_batch(1.0 + sc_ref[...], ROW_CHUNK)
    sh = _per_batch(sh_ref[...], ROW_CHUNK)

    def chunk(r0):
        x = x_ref[pl.ds(r0, ROW_CHUNK), :]
        ms = jnp.mean(x * x, axis=-1, keepdims=True)
        h = x * lax.rsqrt(ms + EPS) * g
        h_scr[pl.ds(r0, ROW_CHUNK), :] = (h * sc + sh).astype(BF16)
    _for_rows(tm, ROW_CHUNK, chunk, unroll=4)

    for n0 in range(0, w_ref.shape[1], tn):
        o_ref[:, n0:n0 + tn] = jnp.dot(
            h_scr[...], w_ref[:, n0:n0 + tn], preferred_element_type=F32).astype(o_ref.dtype)


def _inproj(x, g, scale, shift, w_bf16):
    batch_major = x.ndim == 3
    d = x.shape[-1]
    rows = x.size // d
    n = w_bf16.shape[1]
    tm = min(512, rows)
    tn = min(1024, n)
    row_spec = lambda w: pl.BlockSpec((tm, w), lambda i: (i, 0))
    if batch_major:
        x_spec = pl.BlockSpec((SUBLANES, tm // SUBLANES, d), lambda i: (0, i, 0))
        out_specs = [row_spec(n), row_spec(d)]
        out_shape = [jax.ShapeDtypeStruct((rows, n), BF16), jax.ShapeDtypeStruct((rows, d), F32)]
    else:
        x_spec = row_spec(d)
        out_specs = row_spec(n)
        out_shape = jax.ShapeDtypeStruct((rows, n), BF16)
    return pl.pallas_call(
        functools.partial(_inproj_kernel, tn=tn, batch_major=batch_major),
        grid=(rows // tm,),
        in_specs=[x_spec,
                  pl.BlockSpec((1, d), lambda i: (0, 0)),
                  pl.BlockSpec((SUBLANES, d), lambda i: (0, 0)),
                  pl.BlockSpec((SUBLANES, d), lambda i: (0, 0)),
                  pl.BlockSpec((d, n), lambda i: (0, 0))],
        out_specs=out_specs,
        out_shape=out_shape,
        scratch_shapes=[pltpu.VMEM((tm, d), BF16)],
        compiler_params=_params(("arbitrary",)),
    )(x, g.reshape(1, d), scale, shift, w_bf16)


def _conv_kernel(v_ref, gt_ref, w_ref, cb_ref, lg_ref, lb_ref, og_ref, o_ref, z_scr, a_scr, *, width):
    rows, ch = v_ref.shape
    halo = (width - 1) * SUBLANES

    @pl.when(pl.program_id(0) == 0)
    def _():
        z_scr[pl.ds(0, halo), :] = jnp.zeros((halo, ch), F32)

    def glu(r0):
        v = v_ref[pl.ds(r0, ROW_CHUNK), :].astype(F32)
        gt = gt_ref[pl.ds(r0, ROW_CHUNK), :].astype(F32)
        z_scr[pl.ds(halo + r0, ROW_CHUNK), :] = v * jax.nn.sigmoid(gt)
    _for_rows(rows, ROW_CHUNK, glu)

    lg = lg_ref[...]
    lb = lb_ref[...]
    og = og_ref[...]
    reps = CONV_ROWS // SUBLANES
    lanes = min(CONV_LANES, ch)

    def taps(r0):
        for c0 in range(0, ch, lanes):
            cb = cb_ref[:, c0:c0 + lanes]
            acc = [cb] * reps
            w = {}
            for j in range(reps + width - 1):
                z = z_scr[pl.ds(r0 + j * SUBLANES, SUBLANES), c0:c0 + lanes]
                if j < width:
                    w[j] = w_ref[pl.ds(j * SUBLANES, SUBLANES), c0:c0 + lanes]
                for m in range(reps):
                    k = j - m
                    if 0 <= k < width:
                        acc[m] = acc[m] + w[k] * z
            for m in range(reps):
                a_scr[pl.ds(r0 + m * SUBLANES, SUBLANES), c0:c0 + lanes] = acc[m]
    _for_rows(rows, CONV_ROWS, taps, unroll=1)

    def conv(r0):
        acc = a_scr[pl.ds(r0, ROW_CHUNK), :]
        mu = jnp.mean(acc, axis=-1, keepdims=True)
        cen = acc - mu
        var = jnp.mean(cen * cen, axis=-1, keepdims=True)
        y = cen * lax.rsqrt(var + EPS) * lg + lb
        s = y * jax.nn.sigmoid(y)
        ms = jnp.mean(s * s, axis=-1, keepdims=True)
        o_ref[pl.ds(r0, ROW_CHUNK), :] = (s * lax.rsqrt(ms + EPS) * og).astype(BF16)
    _for_rows(rows, ROW_CHUNK, conv, unroll=8)

    z_scr[pl.ds(0, halo), :] = z_scr[pl.ds(rows, halo), :]


def _conv(p, conv_w, conv_b, ln_g, ln_b, out_g):
    rows = p.shape[0]
    width, ch = conv_w.shape
    halo = (width - 1) * SUBLANES
    tr = min(512, rows)
    assert tr >= halo
    vec = lambda a: a.reshape(1, ch)
    row_spec = lambda col: pl.BlockSpec((tr, ch), lambda i, col=col: (i, col))
    one = pl.BlockSpec((1, ch), lambda i: (0, 0))
    w8 = jnp.broadcast_to(conv_w[:, None, :], (width, SUBLANES, ch)).reshape(width * SUBLANES, ch)
    cb8 = jnp.broadcast_to(conv_b[None, :], (SUBLANES, ch))
    return pl.pallas_call(
        functools.partial(_conv_kernel, width=width),
        grid=(rows // tr,),
        in_specs=[row_spec(0), row_spec(1),
                  pl.BlockSpec((width * SUBLANES, ch), lambda i: (0, 0)),
                  pl.BlockSpec((SUBLANES, ch), lambda i: (0, 0)), one, one, one],
        out_specs=pl.BlockSpec((tr, ch), lambda i: (i, 0)),
        out_shape=jax.ShapeDtypeStruct((rows, ch), BF16),
        scratch_shapes=[pltpu.VMEM((halo + tr, ch), F32), pltpu.VMEM((tr, ch), F32)],
        compiler_params=_params(("arbitrary",)),
    )(p, p, w8, cb8, vec(ln_g), vec(ln_b), vec(out_g))


def _ssm_prep_kernel(lr_ref, li_ref, ldt_ref, br_ref, bi_ref, ar_o, ai_o, bbr_o, bbi_o):
    lr = lr_ref[...]
    li = li_ref[...]
    dt = jnp.exp(ldt_ref[...])
    mag = jnp.exp(lr * dt)
    ang = li * dt
    ar = mag * jnp.cos(ang)
    ai = mag * jnp.sin(ang)
    den = lr * lr + li * li
    nr = ar - 1.0
    ni = ai
    fr = (nr * lr + ni * li) / den
    fi = (ni * lr - nr * li) / den
    br = br_ref[...]
    bi = bi_ref[...]
    ar_o[...] = ar
    ai_o[...] = ai
    bbr_o[...] = fr * br - fi * bi
    bbi_o[...] = fr * bi + fi * br


def _ssm_prep(a_re, a_im, log_dt, b_re, b_im):
    depth, g, p = a_re.shape
    h = b_re.shape[-1]
    rep = lambda a: jnp.broadcast_to(a[:, :, None, :], (depth, g, h, p)).reshape(depth * g * h, p)
    ldt = jnp.broadcast_to(log_dt[:, :, None, None], (depth, g, h, p)).reshape(depth * g * h, p)
    bt = lambda b: jnp.swapaxes(b, -1, -2).reshape(depth * g * h, p)
    n = g * h
    spec = pl.BlockSpec((n, p), lambda l: (l, 0))
    shp = jax.ShapeDtypeStruct((depth * n, p), F32)
    ar, ai, bbr, bbi = pl.pallas_call(
        _ssm_prep_kernel,
        grid=(depth,),
        in_specs=[spec] * 5,
        out_specs=[spec] * 4,
        out_shape=[shp] * 4,
        compiler_params=_params(("arbitrary",)),
    )(rep(a_re), rep(a_im), ldt, bt(b_re), bt(b_im))
    r4 = lambda a: a.reshape(depth, g, h, p)
    return r4(ar)[:, :, 0, :], r4(ai)[:, :, 0, :], r4(bbr), r4(bbi)


def _ssm_block_weights(abar_re, abar_im, bb_re, bb_im, c_re, c_im):
    g, h, p = bb_re.shape
    gpt = min(MXU_DIM // h, g)
    kt = g // gpt
    eye = jnp.eye(gpt, dtype=F32)[None, :, None, :, None]

    def bmat(bb):
        b5 = bb.reshape(kt, gpt, h, 1, p)
        return (b5 * eye).astype(BF16).reshape(kt, gpt * h, gpt * p)

    def cmat(c):
        c5 = jnp.swapaxes(c.reshape(kt, gpt, h, p), 2, 3).reshape(kt, gpt, p, 1, h)
        return (c5 * eye).astype(BF16).reshape(kt, gpt * p, gpt * h)

    bc = jnp.concatenate([bmat(bb_re), bmat(bb_im)], axis=-1)
    cc = jnp.concatenate([cmat(c_re), cmat(-c_im)], axis=1)
    bcast = lambda a: jnp.broadcast_to(a.reshape(1, g * p), (SUBLANES, g * p))
    return bc, cc, bcast(abar_re), bcast(abar_im)


def _ssm_kernel(u_ref, gt_ref, bc_ref, cc_ref, ar_ref, ai_ref, d_ref, og_ref, o_ref,
                s_scr, st_scr, y_scr, *, lane_chunk):
    rows, ch = u_ref.shape
    kt, kch, sw2 = bc_ref.shape
    sw = sw2 // 2
    steps = rows // SUBLANES

    @pl.when(pl.program_id(0) == 0)
    def _():
        st_scr[...] = jnp.zeros(st_scr.shape, F32)

    for k in range(kt):
        ub = u_ref[:, k * kch:(k + 1) * kch].astype(BF16)
        s_scr[:, k * sw2:(k + 1) * sw2] = jnp.dot(ub, bc_ref[k], preferred_element_type=F32)

    for k in range(kt):
        for c in range(sw // lane_chunk):
            re0 = k * sw2 + c * lane_chunk
            im0 = re0 + sw
            a0 = k * sw + c * lane_chunk
            ar = ar_ref[:, a0:a0 + lane_chunk]
            ai = ai_ref[:, a0:a0 + lane_chunk]

            def step(t, carry, re0=re0, im0=im0, ar=ar, ai=ai):
                sr, si = carry
                r = pl.multiple_of(t * SUBLANES, SUBLANES)
                br = s_scr[pl.ds(r, SUBLANES), re0:re0 + lane_chunk]
                bi = s_scr[pl.ds(r, SUBLANES), im0:im0 + lane_chunk]
                nr = ar * sr - ai * si + br
                ni = ar * si + ai * sr + bi
                s_scr[pl.ds(r, SUBLANES), re0:re0 + lane_chunk] = nr
                s_scr[pl.ds(r, SUBLANES), im0:im0 + lane_chunk] = ni
                return nr, ni

            sr, si = lax.fori_loop(
                0, steps, step,
                (st_scr[:, re0:re0 + lane_chunk], st_scr[:, im0:im0 + lane_chunk]))
            st_scr[:, re0:re0 + lane_chunk] = sr
            st_scr[:, im0:im0 + lane_chunk] = si

    for k in range(kt):
        sb = s_scr[:, k * sw2:(k + 1) * sw2].astype(BF16)
        y_scr[:, k * kch:(k + 1) * kch] = jnp.dot(sb, cc_ref[k], preferred_element_type=F32)

    d = d_ref[...]
    og = og_ref[...]

    def gate(r0):
        u = u_ref[pl.ds(r0, ROW_CHUNK), :].astype(F32)
        y = y_scr[pl.ds(r0, ROW_CHUNK), :] + u * d
        v = jax.nn.gelu(y) * jax.nn.sigmoid(gt_ref[pl.ds(r0, ROW_CHUNK), :].astype(F32))
        ms = jnp.mean(v * v, axis=-1, keepdims=True)
        o_ref[pl.ds(r0, ROW_CHUNK), :] = (v * lax.rsqrt(ms + EPS) * og).astype(BF16)
    _for_rows(rows, ROW_CHUNK, gate, unroll=4)


def _ssm(p, col0, bc, cc, ar, ai, d_skip, out_g):
    rows = p.shape[0]
    kt, kch, sw2 = bc.shape
    ch = kt * kch
    tr = min(512, rows)
    lane_chunk = min(512, sw2 // 2)
    vec = lambda a: a.reshape(1, ch)
    row_spec = lambda col: pl.BlockSpec((tr, ch), lambda i, col=col: (i, col))
    one = pl.BlockSpec((1, ch), lambda i: (0, 0))
    full = lambda a: pl.BlockSpec(a.shape, lambda i, nd=a.ndim: (0,) * nd)
    return pl.pallas_call(
        functools.partial(_ssm_kernel, lane_chunk=lane_chunk),
        grid=(rows // tr,),
        in_specs=[row_spec(col0), row_spec(col0 + 1), full(bc), full(cc), full(ar), full(ai), one, one],
        out_specs=pl.BlockSpec((tr, ch), lambda i: (i, 0)),
        out_shape=jax.ShapeDtypeStruct((rows, ch), BF16),
        scratch_shapes=[pltpu.VMEM((tr, kt * sw2), F32),
                        pltpu.VMEM((SUBLANES, kt * sw2), F32),
                        pltpu.VMEM((tr, ch), F32)],
        compiler_params=_params(("arbitrary",)),
    )(p, p, bc, cc, ar, ai, vec(d_skip), vec(out_g))


def _outproj_kernel(yc_ref, ys_ref, wo_ref, x_ref, g1_ref, n2_ref, sc_ref, sh_ref, wr_ref, br_ref,
                    x1_ref, h2_ref, lp_ref, lpt_ref, tab_ref, *, n_groups, per_group):
    tm = x_ref.shape[0]

    x1_ref[...] = (jnp.dot(yc_ref[...], wo_ref[0], preferred_element_type=F32)
                   + jnp.dot(ys_ref[...], wo_ref[1], preferred_element_type=F32))

    g1 = _per_batch(g1_ref[...], ROW_CHUNK)
    sc = _per_batch(1.0 + sc_ref[...], ROW_CHUNK)
    sh = _per_batch(sh_ref[...], ROW_CHUNK)
    n2 = n2_ref[...]

    def chunk(r0):
        x1 = x_ref[pl.ds(r0, ROW_CHUNK), :] + g1 * x1_ref[pl.ds(r0, ROW_CHUNK), :]
        x1_ref[pl.ds(r0, ROW_CHUNK), :] = x1
        ms = jnp.mean(x1 * x1, axis=-1, keepdims=True)
        h2 = x1 * lax.rsqrt(ms + EPS) * n2 * sc + sh
        h2_ref[pl.ds(r0, ROW_CHUNK), :] = h2.astype(BF16)
    _for_rows(tm, ROW_CHUNK, chunk, unroll=4)

    lg = jnp.dot(h2_ref[...], wr_ref[...], preferred_element_type=F32) + br_ref[...]
    nl = tab_ref.shape[1]
    lt = jnp.transpose(lg)[0:nl, :]
    sub = lax.broadcasted_iota(I32, (nl, tm), 0)
    neg = jnp.float32(-jnp.inf)
    big = jnp.int32(ROUTE_LANES)

    def first_max(vals):
        m = jnp.max(vals, axis=0, keepdims=True)
        idx = jnp.min(jnp.where(vals == m, sub, big), axis=0, keepdims=True)
        return m, idx

    gl = jnp.where(sub < n_groups, lt, neg)
    gmax, gidx = first_max(gl)
    g_w = 1.0 / jnp.sum(jnp.exp(gl - gmax), axis=0, keepdims=True)
    lo = n_groups + gidx * per_group
    el = jnp.where((sub >= lo) & (sub < lo + per_group), lt, neg)
    v1, i1 = first_max(el)
    v2, i2 = first_max(jnp.where(sub == i1, neg, el))
    e21 = jnp.exp(v2 - v1)
    w1 = g_w / (1.0 + e21)
    w2 = g_w * e21 / (1.0 + e21)

    hit1 = sub == i1
    hit2 = sub == i2
    onehot = jnp.where(hit1 | hit2, 1.0, 0.0)
    row = lax.broadcasted_iota(I32, (tm, tm), 0)
    col = lax.broadcasted_iota(I32, (tm, tm), 1)
    before = jnp.where(row < col, 1.0, 0.0).astype(BF16)
    earlier = jnp.dot(onehot.astype(BF16), before, preferred_element_type=F32)
    count = jnp.sum(onehot, axis=1, keepdims=True).astype(I32)
    blocks = (count + (SUBLANES - 1)) // SUBLANES
    lrow = lax.broadcasted_iota(I32, (ROUTE_LANES, ROUTE_LANES), 0)
    lcol = lax.broadcasted_iota(I32, (ROUTE_LANES, ROUTE_LANES), 1)
    above = jnp.where(lcol < lrow, 1.0, 0.0).astype(BF16)
    blocks_sq = jnp.concatenate(
        [jnp.broadcast_to(blocks.astype(F32), (nl, ROUTE_LANES)),
         jnp.zeros((ROUTE_LANES - nl, ROUTE_LANES), F32)], axis=0).astype(BF16)
    first = jnp.dot(above, blocks_sq, preferred_element_type=F32)[0:nl, 0:1] * SUBLANES
    slot = first + earlier
    lpos1 = jnp.sum(jnp.where(hit1, slot, 0.0), axis=0, keepdims=True)
    lpos2 = jnp.sum(jnp.where(hit2, slot, 0.0), axis=0, keepdims=True)

    sub8 = lax.broadcasted_iota(I32, (SUBLANES, tm), 0)
    rec = jnp.where(sub8 == 0, lpos1, jnp.where(sub8 == 1, lpos2, jnp.where(
        sub8 == 2, w1, jnp.where(sub8 == 3, w2, 0.0))))
    lpt_ref[0] = rec
    lp_ref[...] = jnp.transpose(
        jnp.concatenate([rec, jnp.zeros((ROUTE_LANES - SUBLANES, tm), F32)], axis=0))
    lane = lax.broadcasted_iota(I32, (nl, ROUTE_LANES), 1)
    tab_ref[0] = jnp.where(lane == 0, first.astype(I32), jnp.where(lane == 1, blocks, 0))


def _outproj(yc, ys, wo_bf16, xt, gate1, n2_g, scale2, shift2, wr_bf16, br, n_groups, per_group):
    rows, d = xt.shape
    ch = yc.shape[1]
    tm = min(256, rows)
    row_spec = lambda w: pl.BlockSpec((tm, w), lambda i: (i, 0))
    one = pl.BlockSpec((1, d), lambda i: (0, 0))
    per_b = pl.BlockSpec((SUBLANES, d), lambda i: (0, 0))
    kern = functools.partial(_outproj_kernel, n_groups=n_groups, per_group=per_group)
    nl = n_groups + n_groups * per_group
    nl += -nl % SUBLANES
    return pl.pallas_call(
        kern,
        grid=(rows // tm,),
        in_specs=[row_spec(ch), row_spec(ch),
                  pl.BlockSpec((2, ch, d), lambda i: (0, 0, 0)),
                  row_spec(d), per_b, one, per_b, per_b,
                  pl.BlockSpec((d, ROUTE_LANES), lambda i: (0, 0)),
                  pl.BlockSpec((1, ROUTE_LANES), lambda i: (0, 0))],
        out_specs=[row_spec(d), row_spec(d), row_spec(ROUTE_LANES),
                   pl.BlockSpec((1, SUBLANES, tm), lambda i: (i, 0, 0)),
                   pl.BlockSpec((1, nl, ROUTE_LANES), lambda i: (i, 0, 0))],
        out_shape=[jax.ShapeDtypeStruct((rows, d), F32),
                   jax.ShapeDtypeStruct((rows, d), BF16),
                   jax.ShapeDtypeStruct((rows, ROUTE_LANES), F32),
                   jax.ShapeDtypeStruct((rows // tm, SUBLANES, tm), F32),
                   jax.ShapeDtypeStruct((rows // tm, nl, ROUTE_LANES), I32)],
        compiler_params=_params(("arbitrary",)),
    )(yc, ys, wo_bf16, xt, gate1, n2_g.reshape(1, d), scale2, shift2, wr_bf16, br)


def _for_blocks(tab_ref, tile, n_experts, fn):
    for e in range(n_experts):
        lrow0 = tab_ref[tile, e]
        grow0 = tab_ref[tile, 2 * n_experts + e]

        def block(c, carry, lrow0=lrow0, grow0=grow0):
            off = c * SUBLANES
            fn(pl.multiple_of(lrow0 + off, SUBLANES), pl.multiple_of(grow0 + off, SUBLANES))
            return carry
        lax.fori_loop(0, tab_ref[tile, n_experts + e], block, 0)


def _dispatch_kernel(zrow_ref, zcnt_ref, tab_ref, lpt_ref, lp_ref, h_ref, hs_ref,
                     sorted_scr, zero_scr, sem, zsem, *, n_experts):
    tm, d = h_ref.shape
    n_sorted = sorted_scr.shape[1]
    i = pl.program_id(0)
    slot = i % 2

    def each_clear(fn):
        big = zero_scr.shape[0]
        per_big = big // SUBLANES
        for e in range(n_experts + 1):
            row0 = zrow_ref[e]
            n_big = zcnt_ref[e] // per_big
            n_small = zcnt_ref[e] - n_big * per_big
            rest0 = row0 + n_big * big

            def big_block(c, carry, row0=row0):
                fn(pltpu.make_async_copy(
                    zero_scr, hs_ref.at[pl.ds(pl.multiple_of(row0 + c * big, SUBLANES), big)], zsem))
                return carry
            lax.fori_loop(0, n_big, big_block, 0)

            def small_block(c, carry, rest0=rest0):
                fn(pltpu.make_async_copy(
                    zero_scr.at[pl.ds(0, SUBLANES)],
                    hs_ref.at[pl.ds(pl.multiple_of(rest0 + c * SUBLANES, SUBLANES), SUBLANES)], zsem))
                return carry
            lax.fori_loop(0, n_small, small_block, 0)

    @pl.when(i == 0)
    def _():
        zero_scr[...] = jnp.zeros(zero_scr.shape, F32)
        each_clear(lambda cp: cp.start())

    srow = lax.broadcasted_iota(I32, (n_sorted, tm), 0)
    p1 = jnp.where(srow == lpt_ref[0, 0:1, :].astype(I32), 1.0, 0.0)
    p2 = jnp.where(srow == lpt_ref[0, 1:2, :].astype(I32), 1.0, 0.0)
    sorted_scr[slot, :, 0:d] = jnp.dot((p1 + p2).astype(BF16), h_ref[...], preferred_element_type=F32)

    lane = lax.broadcasted_iota(I32, (tm, ROUTE_LANES), 1)

    def pieces(w):
        hi = w.astype(BF16).astype(F32)
        mid = (w - hi).astype(BF16).astype(F32)
        lo = w - hi - mid
        return jnp.where(lane == 0, hi, jnp.where(lane == 1, mid, jnp.where(lane == 2, lo, 0.0))).astype(BF16)

    lp = lp_ref[...]
    sorted_scr[slot, :, d:d + ROUTE_LANES] = (
        jnp.dot(p1.astype(BF16), pieces(lp[:, 2:3]), preferred_element_type=F32)
        + jnp.dot(p2.astype(BF16), pieces(lp[:, 3:4]), preferred_element_type=F32))

    def copy(buf):
        return lambda lrow, grow: pltpu.make_async_copy(
            sorted_scr.at[buf, pl.ds(lrow, SUBLANES)], hs_ref.at[pl.ds(grow, SUBLANES)], sem.at[buf])

    _for_blocks(tab_ref, i, n_experts, lambda lrow, grow: copy(slot)(lrow, grow).start())

    @pl.when(i > 0)
    def _():
        _for_blocks(tab_ref, i - 1, n_experts, lambda lrow, grow: copy(1 - slot)(lrow, grow).wait())

    @pl.when(i == pl.num_programs(0) - 1)
    def _():
        _for_blocks(tab_ref, i, n_experts, lambda lrow, grow: copy(slot)(lrow, grow).wait())
        each_clear(lambda cp: cp.wait())


def _dispatch(h2, lp, lpt, tab, zrow, zcnt, n_slots):
    rows, d = h2.shape
    tm = lpt.shape[-1]
    n_experts = zrow.shape[0] - 1
    n_sorted = 2 * tm + n_experts * (SUBLANES - 1)
    n_sorted += -n_sorted % SUBLANES
    width = d + ROUTE_LANES
    grid_spec = pltpu.PrefetchScalarGridSpec(
        num_scalar_prefetch=3,
        grid=(rows // tm,),
        in_specs=[pl.BlockSpec((1, SUBLANES, tm), lambda i, *_: (i, 0, 0)),
                  pl.BlockSpec((tm, ROUTE_LANES), lambda i, *_: (i, 0)),
                  pl.BlockSpec((tm, d), lambda i, *_: (i, 0))],
        out_specs=pl.BlockSpec(memory_space=pl.ANY),
        scratch_shapes=[pltpu.VMEM((2, n_sorted, width), F32), pltpu.VMEM((ZERO_ROWS, width), F32),
                        pltpu.SemaphoreType.DMA((2,)), pltpu.SemaphoreType.DMA(())],
    )
    return pl.pallas_call(
        functools.partial(_dispatch_kernel, n_experts=n_experts),
        grid_spec=grid_spec,
        out_shape=jax.ShapeDtypeStruct((n_slots, width), F32),
        compiler_params=_params(("arbitrary",)),
    )(zrow, zcnt, tab, lpt, lp, h2)


def _expert_kernel(te_ref, nv_ref, hs_ref, wg_ref, wu_ref, wd_ref, o_ref, wg_scr, wu_scr, wd_scr):
    i = pl.program_id(0)

    @pl.when(i < nv_ref[0])
    def _():
        @pl.when((i == 0) | (te_ref[i] != te_ref[jnp.maximum(i - 1, 0)]))
        def _():
            wg_scr[...] = wg_ref[...].astype(BF16)
            wu_scr[...] = wu_ref[...].astype(BF16)
            wd_scr[...] = wd_ref[...].astype(BF16)

        d = wg_scr.shape[0]
        hb = hs_ref[:, 0:d].astype(BF16)
        w = hs_ref[:, d:d + 1] + hs_ref[:, d + 1:d + 2] + hs_ref[:, d + 2:d + 3]
        a = jnp.dot(hb, wg_scr[...], preferred_element_type=F32)
        u = jnp.dot(hb, wu_scr[...], preferred_element_type=F32)
        hm = (a * jax.nn.sigmoid(a) * u * w).astype(BF16)
        o_ref[...] = jnp.dot(hm, wd_scr[...], preferred_element_type=F32)

    @pl.when(i >= nv_ref[0])
    def _():
        o_ref[...] = jnp.zeros(o_ref.shape, F32)


def _experts(hs, w_gate, w_up, w_down, layer, tile_expert, n_valid, tile_e):
    n_slots, width = hs.shape
    d = width - ROUTE_LANES
    per_group = w_gate.shape[2]
    f = w_gate.shape[-1]
    n_tiles = n_slots // tile_e

    def row_map(i, te, nv):
        return (jnp.minimum(i, nv[0] - 1), 0)

    def w_map(i, te, nv):
        e = te[i]
        return (layer, e // per_group, e % per_group, 0, 0)

    grid_spec = pltpu.PrefetchScalarGridSpec(
        num_scalar_prefetch=2,
        grid=(n_tiles,),
        in_specs=[pl.BlockSpec((tile_e, width), row_map),
                  pl.BlockSpec((None, None, None, d, f), w_map),
                  pl.BlockSpec((None, None, None, d, f), w_map),
                  pl.BlockSpec((None, None, None, f, d), w_map)],
        out_specs=pl.BlockSpec((tile_e, d), lambda i, te, nv: (i, 0)),
        scratch_shapes=[pltpu.VMEM((d, f), BF16), pltpu.VMEM((d, f), BF16), pltpu.VMEM((f, d), BF16)],
    )
    return pl.pallas_call(
        _expert_kernel,
        grid_spec=grid_spec,
        out_shape=jax.ShapeDtypeStruct((n_slots, d), F32),
        compiler_params=_params(("arbitrary",)),
    )(tile_expert, n_valid, hs, w_gate, w_up, w_down)


def _combine_kernel(tab_ref, eo_ref, lp_ref, x_ref, g2_ref, gf_ref, o_ref, sorted_scr, y_scr, sem,
                    *, n_experts, final):
    tm, d = x_ref.shape
    n_sorted = sorted_scr.shape[1]
    i = pl.program_id(0)
    slot = i % 2

    def fetch(buf):
        return lambda lrow, grow: pltpu.make_async_copy(
            eo_ref.at[pl.ds(grow, SUBLANES)], sorted_scr.at[buf, pl.ds(lrow, SUBLANES)], sem.at[buf])

    @pl.when(i == 0)
    def _():
        for buf in range(2):
            sorted_scr[buf, pl.ds(2 * tm, n_sorted - 2 * tm), :] = jnp.zeros((n_sorted - 2 * tm, d), F32)
        _for_blocks(tab_ref, 0, n_experts, lambda lrow, grow: fetch(0)(lrow, grow).start())

    @pl.when(i + 1 < pl.num_programs(0))
    def _():
        _for_blocks(tab_ref, i + 1, n_experts, lambda lrow, grow: fetch(1 - slot)(lrow, grow).start())

    _for_blocks(tab_ref, i, n_experts, lambda lrow, grow: fetch(slot)(lrow, grow).wait())

    lp = lp_ref[...]
    scol = lax.broadcasted_iota(I32, (tm, n_sorted), 1)
    q = (jnp.where(scol == lp[:, 0:1].astype(I32), 1.0, 0.0)
         + jnp.where(scol == lp[:, 1:2].astype(I32), 1.0, 0.0)).astype(BF16)
    y_scr[...] = jnp.dot(q, sorted_scr[slot].astype(BF16), preferred_element_type=F32)

    g2 = _per_batch(g2_ref[...], ROW_CHUNK)
    gf = gf_ref[...]

    def chunk(r0):
        x2 = x_ref[pl.ds(r0, ROW_CHUNK), :] + g2 * y_scr[pl.ds(r0, ROW_CHUNK), :]
        if final:
            ms = jnp.mean(x2 * x2, axis=-1, keepdims=True)
            y_scr[pl.ds(r0, ROW_CHUNK), :] = x2 * lax.rsqrt(ms + EPS) * gf
        else:
            o_ref[pl.ds(r0, ROW_CHUNK), :] = x2
    _for_rows(tm, ROW_CHUNK, chunk, unroll=4)

    if final:
        o_ref[...] = jnp.swapaxes(y_scr[...].reshape(tm // SUBLANES, SUBLANES, d), 0, 1)


def _combine(eo, tab, lp, x1, gate2, final_g, n_experts, final):
    rows, d = x1.shape
    tm = rows // tab.shape[0]
    n_sorted = 2 * tm + n_experts * (SUBLANES - 1)
    n_sorted += -n_sorted % SUBLANES
    if final:
        out_spec = pl.BlockSpec((SUBLANES, tm // SUBLANES, d), lambda i, *_: (0, i, 0))
        out_shape = jax.ShapeDtypeStruct((SUBLANES, rows // SUBLANES, d), F32)
    else:
        out_spec = pl.BlockSpec((tm, d), lambda i, *_: (i, 0))
        out_shape = jax.ShapeDtypeStruct((rows, d), F32)
    grid_spec = pltpu.PrefetchScalarGridSpec(
        num_scalar_prefetch=1,
        grid=(rows // tm,),
        in_specs=[pl.BlockSpec(memory_space=pl.ANY),
                  pl.BlockSpec((tm, ROUTE_LANES), lambda i, *_: (i, 0)),
                  pl.BlockSpec((tm, d), lambda i, *_: (i, 0)),
                  pl.BlockSpec((SUBLANES, d), lambda i, *_: (0, 0)),
                  pl.BlockSpec((1, d), lambda i, *_: (0, 0))],
        out_specs=out_spec,
        scratch_shapes=[pltpu.VMEM((2, n_sorted, d), F32), pltpu.VMEM((tm, d), F32),
                        pltpu.SemaphoreType.DMA((2,))],
    )
    return pl.pallas_call(
        functools.partial(_combine_kernel, n_experts=n_experts, final=final),
        grid_spec=grid_spec,
        out_shape=out_shape,
        compiler_params=_params(("arbitrary",)),
    )(tab, eo, lp, x1, gate2, final_g.reshape(1, d))


def _slots(tabs, rows, n_groups, n_experts, tile_e):
    n_tok_tiles = tabs.shape[0]
    first = tabs[:, n_groups:n_groups + n_experts, 0]
    blocks = tabs[:, n_groups:n_groups + n_experts, 1]
    run = blocks * SUBLANES
    total = jnp.sum(run, axis=0)
    region = ((total + tile_e - 1) // tile_e) * tile_e
    ends = jnp.cumsum(region)
    starts = ends - region
    gfirst = starts[None, :] + jnp.cumsum(run, axis=0) - run
    tab = jnp.concatenate([first, blocks, gfirst], axis=1).astype(I32)
    max_rows = 2 * rows + n_tok_tiles * n_experts * (SUBLANES - 1) + n_experts * (tile_e - SUBLANES)
    n_tiles = max_rows // tile_e
    n_valid = ends[-1] // tile_e
    tile_start = jnp.minimum(jnp.arange(n_tiles, dtype=I32), jnp.maximum(n_valid - 1, 0)) * tile_e
    tile_expert = jnp.sum((ends[None, :] <= tile_start[:, None]).astype(I32), axis=1)
    tile_expert = jnp.minimum(tile_expert, n_experts - 1).astype(I32)
    n_slots = n_tiles * tile_e
    zrow = jnp.concatenate([starts + total, ends[-1:]]).astype(I32)
    zcnt = (jnp.concatenate([region - total, n_slots - ends[-1:]]) // SUBLANES).astype(I32)
    return tab, tile_expert, n_valid.reshape(1).astype(I32), zrow, zcnt, n_slots


def kernel(x, c, w_ada, b_ada, norm1_g, w_in, conv_w, conv_b, conv_ln_g, conv_ln_b, ssm_a_re, ssm_a_im, ssm_b_re, ssm_b_im, ssm_c_re, ssm_c_im, ssm_d, ssm_log_dt, out_norm_conv_g, out_norm_ssm_g, w_out, norm2_g, router_group_w, router_group_b, router_expert_w, router_expert_b, exp_w_gate, exp_w_up, exp_w_down, final_norm_g):
    bsz, seq, d = x.shape
    assert bsz == SUBLANES
    depth = w_ada.shape[0]
    conv_ch = conv_w.shape[-1]
    ssm_ch = ssm_d.shape[-1]
    assert conv_ch == ssm_ch and w_in.shape[-1] == 2 * conv_ch + 2 * ssm_ch
    n_groups, per_group = router_expert_w.shape[1], router_expert_w.shape[3]
    n_experts = n_groups * per_group
    assert n_groups + n_experts <= ROUTE_LANES
    rows = bsz * seq
    tile_e = min(EXPERT_TILE, rows // 8)

    mod = _ada(c, w_ada, b_ada)
    abar_re, abar_im, bb_re, bb_im = _ssm_prep(ssm_a_re, ssm_a_im, ssm_log_dt, ssm_b_re, ssm_b_im)
    xt = x

    for l in range(depth):
        shift1, scale1, gate1, shift2, scale2, gate2 = [mod[l, :, k * d:(k + 1) * d] for k in range(6)]

        p = _inproj(xt, norm1_g[l], scale1, shift1, w_in[l].astype(BF16))
        if l == 0:
            p, xt = p
        yc = _conv(p, conv_w[l], conv_b[l], conv_ln_g[l], conv_ln_b[l], out_norm_conv_g[l])
        bc, cc, ar, ai = _ssm_block_weights(abar_re[l], abar_im[l], bb_re[l], bb_im[l], ssm_c_re[l], ssm_c_im[l])
        ys = _ssm(p, 2 * conv_ch // ssm_ch, bc, cc, ar, ai, ssm_d[l], out_norm_ssm_g[l])

        wr = jnp.concatenate([router_group_w[l]] + [router_expert_w[l, g] for g in range(n_groups)], axis=-1)
        wr = jnp.pad(wr, ((0, 0), (0, ROUTE_LANES - wr.shape[1]))).astype(BF16)
        br = jnp.concatenate([router_group_b[l], router_expert_b[l].reshape(-1)])
        br = jnp.pad(br, (0, ROUTE_LANES - br.shape[0])).reshape(1, ROUTE_LANES)
        wo = w_out[l].astype(BF16).reshape(2, conv_ch, d)
        x1, h2, lp, lpt, tabs = _outproj(yc, ys, wo, xt, gate1, norm2_g[l], scale2, shift2, wr, br,
                                         n_groups, per_group)

        tab, tile_expert, n_valid, zrow, zcnt, n_slots = _slots(tabs, rows, n_groups, n_experts, tile_e)
        hs = _dispatch(h2, lp, lpt, tab, zrow, zcnt, n_slots)
        eo = _experts(hs, exp_w_gate, exp_w_up, exp_w_down, l, tile_expert, n_valid, tile_e)
        xt = _combine(eo, tab, lp, x1, gate2, final_norm_g, n_experts, final=l == depth - 1)

    return xt
```

```python
import functools

import jax
import jax.numpy as jnp
from jax import lax
from jax.experimental import pallas as pl
from jax.experimental.pallas import tpu as pltpu

EPS = 1e-6
F32 = jnp.float32
BF16 = jnp.bfloat16
I32 = jnp.int32

SUBLANES = 8
LANES = 128
MXU_DIM = 256
VMEM_LIMIT = 56 * 1024 * 1024
ROUTE_LANES = LANES
ROW_CHUNK = 16
CONV_ROWS = 32
CONV_LANES = 512
EXPERT_TILE = 512
ZERO_ROWS = 256


def _params(sem):
    return pltpu.CompilerParams(dimension_semantics=sem, vmem_limit_bytes=VMEM_LIMIT)


def _for_rows(n_rows, chunk, body, unroll=2):
    def step(i, carry):
        body(pl.multiple_of(i * chunk, chunk))
        return carry
    lax.fori_loop(0, n_rows // chunk, step, 0, unroll=unroll)


def _per_batch(v, rows):
    return jnp.tile(v, (rows // SUBLANES, 1))


def _ada_kernel(c_ref, w_ref, b_ref, o_ref):
    c = c_ref[...]
    ca = (c * jax.nn.sigmoid(c)).astype(BF16)
    o_ref[...] = jnp.dot(ca, w_ref[...].astype(BF16), preferred_element_type=F32) + b_ref[...]


def _ada(c, w_ada, b_ada):
    depth, d, n = w_ada.shape
    bsz = c.shape[0]
    tn = min(512, n)
    return pl.pallas_call(
        _ada_kernel,
        grid=(depth, n // tn),
        in_specs=[pl.BlockSpec((bsz, d), lambda l, j: (0, 0)),
                  pl.BlockSpec((None, d, tn), lambda l, j: (l, 0, j)),
                  pl.BlockSpec((None, 1, tn), lambda l, j: (l, 0, j))],
        out_specs=pl.BlockSpec((None, bsz, tn), lambda l, j: (l, 0, j)),
        out_shape=jax.ShapeDtypeStruct((depth, bsz, n), F32),
        compiler_params=_params(("arbitrary", "arbitrary")),
    )(c, w_ada, b_ada.reshape(depth, 1, n))


def _inproj_kernel(x_ref, g_ref, sc_ref, sh_ref, w_ref, *rest, tn, batch_major):
    if batch_major:
        o_ref, xt_ref, h_scr = rest
        xt_ref[...] = jnp.swapaxes(x_ref[...], 0, 1).reshape(xt_ref.shape)
        x_ref = xt_ref
    else:
        o_ref, h_scr = rest
    tm = x_ref.shape[0]
    g = g_ref[...]
    sc = _per_batch(1.0 + sc_ref[...], ROW_CHUNK)
    sh = _per_batch(sh_ref[...], ROW_CHUNK)

    def chunk(r0):
        x = x_ref[pl.ds(r0, ROW_CHUNK), :]
        ms = jnp.mean(x * x, axis=-1, keepdims=True)
        h = x * lax.rsqrt(ms + EPS) * g
        h_scr[pl.ds(r0, ROW_CHUNK), :] = (h * sc + sh).astype(BF16)
    _for_rows(tm, ROW_CHUNK, chunk, unroll=4)

    for n0 in range(0, w_ref.shape[1], tn):
        o_ref[:, n0:n0 + tn] = jnp.dot(
            h_scr[...], w_ref[:, n0:n0 + tn], preferred_element_type=F32).astype(o_ref.dtype)


def _inproj(x, g, scale, shift, w_bf16):
    batch_major = x.ndim == 3
    d = x.shape[-1]
    rows = x.size // d
    n = w_bf16.shape[1]
    tm = min(512, rows)
    tn = min(1024, n)
    row_spec = lambda w: pl.BlockSpec((tm, w), lambda i: (i, 0))
    if batch_major:
        x_spec = pl.BlockSpec((SUBLANES, tm // SUBLANES, d), lambda i: (0, i, 0))
        out_specs = [row_spec(n), row_spec(d)]
        out_shape = [jax.ShapeDtypeStruct((rows, n), BF16), jax.ShapeDtypeStruct((rows, d), F32)]
    else:
        x_spec = row_spec(d)
        out_specs = row_spec(n)
        out_shape = jax.ShapeDtypeStruct((rows, n), BF16)
    return pl.pallas_call(
        functools.partial(_inproj_kernel, tn=tn, batch_major=batch_major),
        grid=(rows // tm,),
        in_specs=[x_spec,
                  pl.BlockSpec((1, d), lambda i: (0, 0)),
                  pl.BlockSpec((SUBLANES, d), lambda i: (0, 0)),
                  pl.BlockSpec((SUBLANES, d), lambda i: (0, 0)),
                  pl.BlockSpec((d, n), lambda i: (0, 0))],
        out_specs=out_specs,
        out_shape=out_shape,
        scratch_shapes=[pltpu.VMEM((tm, d), BF16)],
        compiler_params=_params(("arbitrary",)),
    )(x, g.reshape(1, d), scale, shift, w_bf16)


def _conv_kernel(v_ref, gt_ref, w_ref, cb_ref, lg_ref, lb_ref, og_ref, o_ref, z_scr, a_scr, *, width):
    rows, ch = v_ref.shape
    halo = (width - 1) * SUBLANES

    @pl.when(pl.program_id(0) == 0)
    def _():
        z_scr[pl.ds(0, halo), :] = jnp.zeros((halo, ch), F32)

    def glu(r0):
        v = v_ref[pl.ds(r0, ROW_CHUNK), :].astype(F32)
        gt = gt_ref[pl.ds(r0, ROW_CHUNK), :].astype(F32)
        z_scr[pl.ds(halo + r0, ROW_CHUNK), :] = v * jax.nn.sigmoid(gt)
    _for_rows(rows, ROW_CHUNK, glu)

    lg = lg_ref[...]
    lb = lb_ref[...]
    og = og_ref[...]
    reps = CONV_ROWS // SUBLANES
    lanes = min(CONV_LANES, ch)

    def taps(r0):
        for c0 in range(0, ch, lanes):
            cb = cb_ref[:, c0:c0 + lanes]
            acc = [cb] * reps
            w = {}
            for j in range(reps + width - 1):
                z = z_scr[pl.ds(r0 + j * SUBLANES, SUBLANES), c0:c0 + lanes]
                if j < width:
                    w[j] = w_ref[pl.ds(j * SUBLANES, SUBLANES), c0:c0 + lanes]
                for m in range(reps):
                    k = j - m
                    if 0 <= k < width:
                        acc[m] = acc[m] + w[k] * z
            for m in range(reps):
                a_scr[pl.ds(r0 + m * SUBLANES, SUBLANES), c0:c0 + lanes] = acc[m]
    _for_rows(rows, CONV_ROWS, taps, unroll=1)

    def conv(r0):
        acc = a_scr[pl.ds(r0, ROW_CHUNK), :]
        mu = jnp.mean(acc, axis=-1, keepdims=True)
        cen = acc - mu
        var = jnp.mean(cen * cen, axis=-1, keepdims=True)
        y = cen * lax.rsqrt(var + EPS) * lg + lb
        s = y * jax.nn.sigmoid(y)
        ms = jnp.mean(s * s, axis=-1, keepdims=True)
        o_ref[pl.ds(r0, ROW_CHUNK), :] = (s * lax.rsqrt(ms + EPS) * og).astype(BF16)
    _for_rows(rows, ROW_CHUNK, conv, unroll=8)

    z_scr[pl.ds(0, halo), :] = z_scr[pl.ds(rows, halo), :]


def _conv(p, conv_w, conv_b, ln_g, ln_b, out_g):
    rows = p.shape[0]
    width, ch = conv_w.shape
    halo = (width - 1) * SUBLANES
    tr = min(512, rows)
    assert tr >= halo
    vec = lambda a: a.reshape(1, ch)
    row_spec = lambda col: pl.BlockSpec((tr, ch), lambda i, col=col: (i, col))
    one = pl.BlockSpec((1, ch), lambda i: (0, 0))
    w8 = jnp.broadcast_to(conv_w[:, None, :], (width, SUBLANES, ch)).reshape(width * SUBLANES, ch)
    cb8 = jnp.broadcast_to(conv_b[None, :], (SUBLANES, ch))
    return pl.pallas_call(
        functools.partial(_conv_kernel, width=width),
        grid=(rows // tr,),
        in_specs=[row_spec(0), row_spec(1),
                  pl.BlockSpec((width * SUBLANES, ch), lambda i: (0, 0)),
                  pl.BlockSpec((SUBLANES, ch), lambda i: (0, 0)), one, one, one],
        out_specs=pl.BlockSpec((tr, ch), lambda i: (i, 0)),
        out_shape=jax.ShapeDtypeStruct((rows, ch), BF16),
        scratch_shapes=[pltpu.VMEM((halo + tr, ch), F32), pltpu.VMEM((tr, ch), F32)],
        compiler_params=_params(("arbitrary",)),
    )(p, p, w8, cb8, vec(ln_g), vec(ln_b), vec(out_g))


def _ssm_prep_kernel(lr_ref, li_ref, ldt_ref, br_ref, bi_ref, ar_o, ai_o, bbr_o, bbi_o):
    lr = lr_ref[...]
    li = li_ref[...]
    dt = jnp.exp(ldt_ref[...])
    mag = jnp.exp(lr * dt)
    ang = li * dt
    ar = mag * jnp.cos(ang)
    ai = mag * jnp.sin(ang)
    den = lr * lr + li * li
    nr = ar - 1.0
    ni = ai
    fr = (nr * lr + ni * li) / den
    fi = (ni * lr - nr * li) / den
    br = br_ref[...]
    bi = bi_ref[...]
    ar_o[...] = ar
    ai_o[...] = ai
    bbr_o[...] = fr * br - fi * bi
    bbi_o[...] = fr * bi + fi * br


def _ssm_prep(a_re, a_im, log_dt, b_re, b_im):
    depth, g, p = a_re.shape
    h = b_re.shape[-1]
    rep = lambda a: jnp.broadcast_to(a[:, :, None, :], (depth, g, h, p)).reshape(depth * g * h, p)
    ldt = jnp.broadcast_to(log_dt[:, :, None, None], (depth, g, h, p)).reshape(depth * g * h, p)
    bt = lambda b: jnp.swapaxes(b, -1, -2).reshape(depth * g * h, p)
    n = g * h
    spec = pl.BlockSpec((n, p), lambda l: (l, 0))
    shp = jax.ShapeDtypeStruct((depth * n, p), F32)
    ar, ai, bbr, bbi = pl.pallas_call(
        _ssm_prep_kernel,
        grid=(depth,),
        in_specs=[spec] * 5,
        out_specs=[spec] * 4,
        out_shape=[shp] * 4,
        compiler_params=_params(("arbitrary",)),
    )(rep(a_re), rep(a_im), ldt, bt(b_re), bt(b_im))
    r4 = lambda a: a.reshape(depth, g, h, p)
    return r4(ar)[:, :, 0, :], r4(ai)[:, :, 0, :], r4(bbr), r4(bbi)


def _ssm_block_weights(abar_re, abar_im, bb_re, bb_im, c_re, c_im):
    g, h, p = bb_re.shape
    gpt = min(MXU_DIM // h, g)
    kt = g // gpt
    eye = jnp.eye(gpt, dtype=F32)[None, :, None, :, None]

    def bmat(bb):
        b5 = bb.reshape(kt, gpt, h, 1, p)
        return (b5 * eye).astype(BF16).reshape(kt, gpt * h, gpt * p)

    def cmat(c):
        c5 = jnp.swapaxes(c.reshape(kt, gpt, h, p), 2, 3).reshape(kt, gpt, p, 1, h)
        return (c5 * eye).astype(BF16).reshape(kt, gpt * p, gpt * h)

    bc = jnp.concatenate([bmat(bb_re), bmat(bb_im)], axis=-1)
    cc = jnp.concatenate([cmat(c_re), cmat(-c_im)], axis=1)
    bcast = lambda a: jnp.broadcast_to(a.reshape(1, g * p), (SUBLANES, g * p))
    return bc, cc, bcast(abar_re), bcast(abar_im)


def _ssm_kernel(u_ref, gt_ref, bc_ref, cc_ref, ar_ref, ai_ref, d_ref, og_ref, o_ref,
                s_scr, st_scr, y_scr, *, lane_chunk):
    rows, ch = u_ref.shape
    kt, kch, sw2 = bc_ref.shape
    sw = sw2 // 2
    steps = rows // SUBLANES

    @pl.when(pl.program_id(0) == 0)
    def _():
        st_scr[...] = jnp.zeros(st_scr.shape, F32)

    for k in range(kt):
        ub = u_ref[:, k * kch:(k + 1) * kch].astype(BF16)
        s_scr[:, k * sw2:(k + 1) * sw2] = jnp.dot(ub, bc_ref[k], preferred_element_type=F32)

    for k in range(kt):
        for c in range(sw // lane_chunk):
            re0 = k * sw2 + c * lane_chunk
            im0 = re0 + sw
            a0 = k * sw + c * lane_chunk
            ar = ar_ref[:, a0:a0 + lane_chunk]
            ai = ai_ref[:, a0:a0 + lane_chunk]

            def step(t, carry, re0=re0, im0=im0, ar=ar, ai=ai):
                sr, si = carry
                r = pl.multiple_of(t * SUBLANES, SUBLANES)
                br = s_scr[pl.ds(r, SUBLANES), re0:re0 + lane_chunk]
                bi = s_scr[pl.ds(r, SUBLANES), im0:im0 + lane_chunk]
                nr = ar * sr - ai * si + br
                ni = ar * si + ai * sr + bi
                s_scr[pl.ds(r, SUBLANES), re0:re0 + lane_chunk] = nr
                s_scr[pl.ds(r, SUBLANES), im0:im0 + lane_chunk] = ni
                return nr, ni

            sr, si = lax.fori_loop(
                0, steps, step,
                (st_scr[:, re0:re0 + lane_chunk], st_scr[:, im0:im0 + lane_chunk]))
            st_scr[:, re0:re0 + lane_chunk] = sr
            st_scr[:, im0:im0 + lane_chunk] = si

    for k in range(kt):
        sb = s_scr[:, k * sw2:(k + 1) * sw2].astype(BF16)
        y_scr[:, k * kch:(k + 1) * kch] = jnp.dot(sb, cc_ref[k], preferred_element_type=F32)

    d = d_ref[...]
    og = og_ref[...]

    def gate(r0):
        u = u_ref[pl.ds(r0, ROW_CHUNK), :].astype(F32)
        y = y_scr[pl.ds(r0, ROW_CHUNK), :] + u * d
        v = jax.nn.gelu(y) * jax.nn.sigmoid(gt_ref[pl.ds(r0, ROW_CHUNK), :].astype(F32))
        ms = jnp.mean(v * v, axis=-1, keepdims=True)
        o_ref[pl.ds(r0, ROW_CHUNK), :] = (v * lax.rsqrt(ms + EPS) * og).astype(BF16)
    _for_rows(rows, ROW_CHUNK, gate, unroll=4)


def _ssm(p, col0, bc, cc, ar, ai, d_skip, out_g):
    rows = p.shape[0]
    kt, kch, sw2 = bc.shape
    ch = kt * kch
    tr = min(512, rows)
    lane_chunk = min(512, sw2 // 2)
    vec = lambda a: a.reshape(1, ch)
    row_spec = lambda col: pl.BlockSpec((tr, ch), lambda i, col=col: (i, col))
    one = pl.BlockSpec((1, ch), lambda i: (0, 0))
    full = lambda a: pl.BlockSpec(a.shape, lambda i, nd=a.ndim: (0,) * nd)
    return pl.pallas_call(
        functools.partial(_ssm_kernel, lane_chunk=lane_chunk),
        grid=(rows // tr,),
        in_specs=[row_spec(col0), row_spec(col0 + 1), full(bc), full(cc), full(ar), full(ai), one, one],
        out_specs=pl.BlockSpec((tr, ch), lambda i: (i, 0)),
        out_shape=jax.ShapeDtypeStruct((rows, ch), BF16),
        scratch_shapes=[pltpu.VMEM((tr, kt * sw2), F32),
                        pltpu.VMEM((SUBLANES, kt * sw2), F32),
                        pltpu.VMEM((tr, ch), F32)],
        compiler_params=_params(("arbitrary",)),
    )(p, p, bc, cc, ar, ai, vec(d_skip), vec(out_g))


def _outproj_kernel(yc_ref, ys_ref, wo_ref, x_ref, g1_ref, n2_ref, sc_ref, sh_ref, wr_ref, br_ref,
                    x1_ref, h2_ref, lp_ref, lpt_ref, tab_ref, *, n_groups, per_group):
    tm = x_ref.shape[0]

    x1_ref[...] = (jnp.dot(yc_ref[...], wo_ref[0], preferred_element_type=F32)
                   + jnp.dot(ys_ref[...], wo_ref[1], preferred_element_type=F32))

    g1 = _per_batch(g1_ref[...], ROW_CHUNK)
    sc = _per_batch(1.0 + sc_ref[...], ROW_CHUNK)
    sh = _per_batch(sh_ref[...], ROW_CHUNK)
    n2 = n2_ref[...]

    def chunk(r0):
        x1 = x_ref[pl.ds(r0, ROW_CHUNK), :] + g1 * x1_ref[pl.ds(r0, ROW_CHUNK), :]
        x1_ref[pl.ds(r0, ROW_CHUNK), :] = x1
        ms = jnp.mean(x1 * x1, axis=-1, keepdims=True)
        h2 = x1 * lax.rsqrt(ms + EPS) * n2 * sc + sh
        h2_ref[pl.ds(r0, ROW_CHUNK), :] = h2.astype(BF16)
    _for_rows(tm, ROW_CHUNK, chunk, unroll=4)

    lg = jnp.dot(h2_ref[...], wr_ref[...], preferred_element_type=F32) + br_ref[...]
    nl = tab_ref.shape[1]
    lt = jnp.transpose(lg)[0:nl, :]
    sub = lax.broadcasted_iota(I32, (nl, tm), 0)
    neg = jnp.float32(-jnp.inf)
    big = jnp.int32(ROUTE_LANES)

    def first_max(vals):
        m = jnp.max(vals, axis=0, keepdims=True)
        idx = jnp.min(jnp.where(vals == m, sub, big), axis=0, keepdims=True)
        return m, idx

    gl = jnp.where(sub < n_groups, lt, neg)
    gmax, gidx = first_max(gl)
    g_w = 1.0 / jnp.sum(jnp.exp(gl - gmax), axis=0, keepdims=True)
    lo = n_groups + gidx * per_group
    el = jnp.where((sub >= lo) & (sub < lo + per_group), lt, neg)
    v1, i1 = first_max(el)
    v2, i2 = first_max(jnp.where(sub == i1, neg, el))
    e21 = jnp.exp(v2 - v1)
    w1 = g_w / (1.0 + e21)
    w2 = g_w * e21 / (1.0 + e21)

    hit1 = sub == i1
    hit2 = sub == i2
    onehot = jnp.where(hit1 | hit2, 1.0, 0.0)
    row = lax.broadcasted_iota(I32, (tm, tm), 0)
    col = lax.broadcasted_iota(I32, (tm, tm), 1)
    before = jnp.where(row < col, 1.0, 0.0).astype(BF16)
    earlier = jnp.dot(onehot.astype(BF16), before, preferred_element_type=F32)
    count = jnp.sum(onehot, axis=1, keepdims=True).astype(I32)
    blocks = (count + (SUBLANES - 1)) // SUBLANES
    lrow = lax.broadcasted_iota(I32, (ROUTE_LANES, ROUTE_LANES), 0)
    lcol = lax.broadcasted_iota(I32, (ROUTE_LANES, ROUTE_LANES), 1)
    above = jnp.where(lcol < lrow, 1.0, 0.0).astype(BF16)
    blocks_sq = jnp.concatenate(
        [jnp.broadcast_to(blocks.astype(F32), (nl, ROUTE_LANES)),
         jnp.zeros((ROUTE_LANES - nl, ROUTE_LANES), F32)], axis=0).astype(BF16)
    first = jnp.dot(above, blocks_sq, preferred_element_type=F32)[0:nl, 0:1] * SUBLANES
    slot = first + earlier
    lpos1 = jnp.sum(jnp.where(hit1, slot, 0.0), axis=0, keepdims=True)
    lpos2 = jnp.sum(jnp.where(hit2, slot, 0.0), axis=0, keepdims=True)

    sub8 = lax.broadcasted_iota(I32, (SUBLANES, tm), 0)
    rec = jnp.where(sub8 == 0, lpos1, jnp.where(sub8 == 1, lpos2, jnp.where(
        sub8 == 2, w1, jnp.where(sub8 == 3, w2, 0.0))))
    lpt_ref[0] = rec
    lp_ref[...] = jnp.transpose(
        jnp.concatenate([rec, jnp.zeros((ROUTE_LANES - SUBLANES, tm), F32)], axis=0))
    lane = lax.broadcasted_iota(I32, (nl, ROUTE_LANES), 1)
    tab_ref[0] = jnp.where(lane == 0, first.astype(I32), jnp.where(lane == 1, blocks, 0))


def _outproj(yc, ys, wo_bf16, xt, gate1, n2_g, scale2, shift2, wr_bf16, br, n_groups, per_group):
    rows, d = xt.shape
    ch = yc.shape[1]
    tm = min(256, rows)
    row_spec = lambda w: pl.BlockSpec((tm, w), lambda i: (i, 0))
    one = pl.BlockSpec((1, d), lambda i: (0, 0))
    per_b = pl.BlockSpec((SUBLANES, d), lambda i: (0, 0))
    kern = functools.partial(_outproj_kernel, n_groups=n_groups, per_group=per_group)
    nl = n_groups + n_groups * per_group
    nl += -nl % SUBLANES
    return pl.pallas_call(
        kern,
        grid=(rows // tm,),
        in_specs=[row_spec(ch), row_spec(ch),
                  pl.BlockSpec((2, ch, d), lambda i: (0, 0, 0)),
                  row_spec(d), per_b, one, per_b, per_b,
                  pl.BlockSpec((d, ROUTE_LANES), lambda i: (0, 0)),
                  pl.BlockSpec((1, ROUTE_LANES), lambda i: (0, 0))],
        out_specs=[row_spec(d), row_spec(d), row_spec(ROUTE_LANES),
                   pl.BlockSpec((1, SUBLANES, tm), lambda i: (i, 0, 0)),
                   pl.BlockSpec((1, nl, ROUTE_LANES), lambda i: (i, 0, 0))],
        out_shape=[jax.ShapeDtypeStruct((rows, d), F32),
                   jax.ShapeDtypeStruct((rows, d), BF16),
                   jax.ShapeDtypeStruct((rows, ROUTE_LANES), F32),
                   jax.ShapeDtypeStruct((rows // tm, SUBLANES, tm), F32),
                   jax.ShapeDtypeStruct((rows // tm, nl, ROUTE_LANES), I32)],
        compiler_params=_params(("arbitrary",)),
    )(yc, ys, wo_bf16, xt, gate1, n2_g.reshape(1, d), scale2, shift2, wr_bf16, br)


def _for_blocks(tab_ref, tile, n_experts, fn):
    for e in range(n_experts):
        lrow0 = tab_ref[tile, e]
        grow0 = tab_ref[tile, 2 * n_experts + e]

        def block(c, carry, lrow0=lrow0, grow0=grow0):
            off = c * SUBLANES
            fn(pl.multiple_of(lrow0 + off, SUBLANES), pl.multiple_of(grow0 + off, SUBLANES))
            return carry
        lax.fori_loop(0, tab_ref[tile, n_experts + e], block, 0)


def _dispatch_kernel(zrow_ref, zcnt_ref, tab_ref, lpt_ref, lp_ref, h_ref, hs_ref,
                     sorted_scr, zero_scr, sem, zsem, *, n_experts):
    tm, d = h_ref.shape
    n_sorted = sorted_scr.shape[1]
    i = pl.program_id(0)
    slot = i % 2

    def each_clear(fn):
        big = zero_scr.shape[0]
        per_big = big // SUBLANES
        for e in range(n_experts + 1):
            row0 = zrow_ref[e]
            n_big = zcnt_ref[e] // per_big
            n_small = zcnt_ref[e] - n_big * per_big
            rest0 = row0 + n_big * big

            def big_block(c, carry, row0=row0):
                fn(pltpu.make_async_copy(
                    zero_scr, hs_ref.at[pl.ds(pl.multiple_of(row0 + c * big, SUBLANES), big)], zsem))
                return carry
            lax.fori_loop(0, n_big, big_block, 0)

            def small_block(c, carry, rest0=rest0):
                fn(pltpu.make_async_copy(
                    zero_scr.at[pl.ds(0, SUBLANES)],
                    hs_ref.at[pl.ds(pl.multiple_of(rest0 + c * SUBLANES, SUBLANES), SUBLANES)], zsem))
                return carry
            lax.fori_loop(0, n_small, small_block, 0)

    @pl.when(i == 0)
    def _():
        zero_scr[...] = jnp.zeros(zero_scr.shape, F32)
        each_clear(lambda cp: cp.start())

    srow = lax.broadcasted_iota(I32, (n_sorted, tm), 0)
    p1 = jnp.where(srow == lpt_ref[0, 0:1, :].astype(I32), 1.0, 0.0)
    p2 = jnp.where(srow == lpt_ref[0, 1:2, :].astype(I32), 1.0, 0.0)
    sorted_scr[slot, :, 0:d] = jnp.dot((p1 + p2).astype(BF16), h_ref[...], preferred_element_type=F32)

    lane = lax.broadcasted_iota(I32, (tm, ROUTE_LANES), 1)

    def pieces(w):
        hi = w.astype(BF16).astype(F32)
        mid = (w - hi).astype(BF16).astype(F32)
        lo = w - hi - mid
        return jnp.where(lane == 0, hi, jnp.where(lane == 1, mid, jnp.where(lane == 2, lo, 0.0))).astype(BF16)

    lp = lp_ref[...]
    sorted_scr[slot, :, d:d + ROUTE_LANES] = (
        jnp.dot(p1.astype(BF16), pieces(lp[:, 2:3]), preferred_element_type=F32)
        + jnp.dot(p2.astype(BF16), pieces(lp[:, 3:4]), preferred_element_type=F32))

    def copy(buf):
        return lambda lrow, grow: pltpu.make_async_copy(
            sorted_scr.at[buf, pl.ds(lrow, SUBLANES)], hs_ref.at[pl.ds(grow, SUBLANES)], sem.at[buf])

    _for_blocks(tab_ref, i, n_experts, lambda lrow, grow: copy(slot)(lrow, grow).start())

    @pl.when(i > 0)
    def _():
        _for_blocks(tab_ref, i - 1, n_experts, lambda lrow, grow: copy(1 - slot)(lrow, grow).wait())

    @pl.when(i == pl.num_programs(0) - 1)
    def _():
        _for_blocks(tab_ref, i, n_experts, lambda lrow, grow: copy(slot)(lrow, grow).wait())
        each_clear(lambda cp: cp.wait())


def _dispatch(h2, lp, lpt, tab, zrow, zcnt, n_slots):
    rows, d = h2.shape
    tm = lpt.shape[-1]
    n_experts = zrow.shape[0] - 1
    n_sorted = 2 * tm + n_experts * (SUBLANES - 1)
    n_sorted += -n_sorted % SUBLANES
    width = d + ROUTE_LANES
    grid_spec = pltpu.PrefetchScalarGridSpec(
        num_scalar_prefetch=3,
        grid=(rows // tm,),
        in_specs=[pl.BlockSpec((1, SUBLANES, tm), lambda i, *_: (i, 0, 0)),
                  pl.BlockSpec((tm, ROUTE_LANES), lambda i, *_: (i, 0)),
                  pl.BlockSpec((tm, d), lambda i, *_: (i, 0))],
        out_specs=pl.BlockSpec(memory_space=pl.ANY),
        scratch_shapes=[pltpu.VMEM((2, n_sorted, width), F32), pltpu.VMEM((ZERO_ROWS, width), F32),
                        pltpu.SemaphoreType.DMA((2,)), pltpu.SemaphoreType.DMA(())],
    )
    return pl.pallas_call(
        functools.partial(_dispatch_kernel, n_experts=n_experts),
        grid_spec=grid_spec,
        out_shape=jax.ShapeDtypeStruct((n_slots, width), F32),
        compiler_params=_params(("arbitrary",)),
    )(zrow, zcnt, tab, lpt, lp, h2)


def _expert_kernel(te_ref, nv_ref, hs_ref, wg_ref, wu_ref, wd_ref, eo_ref,
                   wg_scr, wu_scr, wd_scr, out_scr, zero_scr, sem, zsem):
    i = pl.program_id(0)
    last = pl.num_programs(0) - 1
    nv = nv_ref[0]
    slot = i % 2
    tile_e = out_scr.shape[1]
    zrows = zero_scr.shape[0]

    def each_clear(fn):
        def block(c, carry):
            row0 = pl.multiple_of(nv * tile_e + c * zrows, zrows)
            fn(pltpu.make_async_copy(zero_scr, eo_ref.at[pl.ds(row0, zrows)], zsem))
            return carry
        lax.fori_loop(0, (pl.num_programs(0) - nv) * (tile_e // zrows), block, 0)

    @pl.when(i == 0)
    def _():
        zero_scr[...] = jnp.zeros(zero_scr.shape, F32)
        each_clear(lambda cp: cp.start())

    def put(buf, tile):
        return pltpu.make_async_copy(
            out_scr.at[buf], eo_ref.at[pl.ds(pl.multiple_of(tile * tile_e, tile_e), tile_e)], sem.at[buf])

    @pl.when(i < nv)
    def _():
        @pl.when((i == 0) | (te_ref[i] != te_ref[jnp.maximum(i - 1, 0)]))
        def _():
            wg_scr[...] = wg_ref[...].astype(BF16)
            wu_scr[...] = wu_ref[...].astype(BF16)
            wd_scr[...] = wd_ref[...].astype(BF16)

        d = wg_scr.shape[0]
        hb = hs_ref[:, 0:d].astype(BF16)
        w = hs_ref[:, d:d + 1] + hs_ref[:, d + 1:d + 2] + hs_ref[:, d + 2:d + 3]
        a = jnp.dot(hb, wg_scr[...], preferred_element_type=F32)
        u = jnp.dot(hb, wu_scr[...], preferred_element_type=F32)
        hm = (a * jax.nn.sigmoid(a) * u * w).astype(BF16)
        out_scr[slot] = jnp.dot(hm, wd_scr[...], preferred_element_type=F32)
        put(slot, i).start()

    @pl.when((i > 0) & (i - 1 < nv))
    def _():
        put(1 - slot, i - 1).wait()

    @pl.when(i == last)
    def _():
        @pl.when(i < nv)
        def _():
            put(slot, i).wait()
        each_clear(lambda cp: cp.wait())


def _experts(hs, w_gate, w_up, w_down, layer, tile_expert, n_valid, tile_e):
    n_slots, width = hs.shape
    d = width - ROUTE_LANES
    per_group = w_gate.shape[2]
    f = w_gate.shape[-1]
    n_tiles = n_slots // tile_e

    def row_map(i, te, nv):
        return (jnp.minimum(i, nv[0] - 1), 0)

    def w_map(i, te, nv):
        e = te[i]
        return (layer, e // per_group, e % per_group, 0, 0)

    grid_spec = pltpu.PrefetchScalarGridSpec(
        num_scalar_prefetch=2,
        grid=(n_tiles,),
        in_specs=[pl.BlockSpec((tile_e, width), row_map),
                  pl.BlockSpec((None, None, None, d, f), w_map),
                  pl.BlockSpec((None, None, None, d, f), w_map),
                  pl.BlockSpec((None, None, None, f, d), w_map)],
        out_specs=pl.BlockSpec(memory_space=pl.ANY),
        scratch_shapes=[pltpu.VMEM((d, f), BF16), pltpu.VMEM((d, f), BF16), pltpu.VMEM((f, d), BF16),
                        pltpu.VMEM((2, tile_e, d), F32), pltpu.VMEM((min(ZERO_ROWS, tile_e), d), F32),
                        pltpu.SemaphoreType.DMA((2,)), pltpu.SemaphoreType.DMA(())],
    )
    return pl.pallas_call(
        _expert_kernel,
        grid_spec=grid_spec,
        out_shape=jax.ShapeDtypeStruct((n_slots, d), F32),
        compiler_params=_params(("arbitrary",)),
    )(tile_expert, n_valid, hs, w_gate, w_up, w_down)


def _combine_kernel(tab_ref, eo_ref, lp_ref, x_ref, g2_ref, gf_ref, o_ref, sorted_scr, y_scr, sem,
                    *, n_experts, final):
    tm, d = x_ref.shape
    n_sorted = sorted_scr.shape[1]
    i = pl.program_id(0)
    slot = i % 2

    def fetch(buf):
        return lambda lrow, grow: pltpu.make_async_copy(
            eo_ref.at[pl.ds(grow, SUBLANES)], sorted_scr.at[buf, pl.ds(lrow, SUBLANES)], sem.at[buf])

    @pl.when(i == 0)
    def _():
        for buf in range(2):
            sorted_scr[buf, pl.ds(2 * tm, n_sorted - 2 * tm), :] = jnp.zeros((n_sorted - 2 * tm, d), F32)
        _for_blocks(tab_ref, 0, n_experts, lambda lrow, grow: fetch(0)(lrow, grow).start())

    @pl.when(i + 1 < pl.num_programs(0))
    def _():
        _for_blocks(tab_ref, i + 1, n_experts, lambda lrow, grow: fetch(1 - slot)(lrow, grow).start())

    _for_blocks(tab_ref, i, n_experts, lambda lrow, grow: fetch(slot)(lrow, grow).wait())

    lp = lp_ref[...]
    scol = lax.broadcasted_iota(I32, (tm, n_sorted), 1)
    q = (jnp.where(scol == lp[:, 0:1].astype(I32), 1.0, 0.0)
         + jnp.where(scol == lp[:, 1:2].astype(I32), 1.0, 0.0)).astype(BF16)
    y_scr[...] = jnp.dot(q, sorted_scr[slot].astype(BF16), preferred_element_type=F32)

    g2 = _per_batch(g2_ref[...], ROW_CHUNK)
    gf = gf_ref[...]

    def chunk(r0):
        x2 = x_ref[pl.ds(r0, ROW_CHUNK), :] + g2 * y_scr[pl.ds(r0, ROW_CHUNK), :]
        if final:
            ms = jnp.mean(x2 * x2, axis=-1, keepdims=True)
            y_scr[pl.ds(r0, ROW_CHUNK), :] = x2 * lax.rsqrt(ms + EPS) * gf
        else:
            o_ref[pl.ds(r0, ROW_CHUNK), :] = x2
    _for_rows(tm, ROW_CHUNK, chunk, unroll=4)

    if final:
        o_ref[...] = jnp.swapaxes(y_scr[...].reshape(tm // SUBLANES, SUBLANES, d), 0, 1)


def _combine(eo, tab, lp, x1, gate2, final_g, n_experts, final):
    rows, d = x1.shape
    tm = rows // tab.shape[0]
    n_sorted = 2 * tm + n_experts * (SUBLANES - 1)
    n_sorted += -n_sorted % SUBLANES
    if final:
        out_spec = pl.BlockSpec((SUBLANES, tm // SUBLANES, d), lambda i, *_: (0, i, 0))
        out_shape = jax.ShapeDtypeStruct((SUBLANES, rows // SUBLANES, d), F32)
    else:
        out_spec = pl.BlockSpec((tm, d), lambda i, *_: (i, 0))
        out_shape = jax.ShapeDtypeStruct((rows, d), F32)
    grid_spec = pltpu.PrefetchScalarGridSpec(
        num_scalar_prefetch=1,
        grid=(rows // tm,),
        in_specs=[pl.BlockSpec(memory_space=pl.ANY),
                  pl.BlockSpec((tm, ROUTE_LANES), lambda i, *_: (i, 0)),
                  pl.BlockSpec((tm, d), lambda i, *_: (i, 0)),
                  pl.BlockSpec((SUBLANES, d), lambda i, *_: (0, 0)),
                  pl.BlockSpec((1, d), lambda i, *_: (0, 0))],
        out_specs=out_spec,
        scratch_shapes=[pltpu.VMEM((2, n_sorted, d), F32), pltpu.VMEM((tm, d), F32),
                        pltpu.SemaphoreType.DMA((2,))],
    )
    return pl.pallas_call(
        functools.partial(_combine_kernel, n_experts=n_experts, final=final),
        grid_spec=grid_spec,
        out_shape=out_shape,
        compiler_params=_params(("arbitrary",)),
    )(tab, eo, lp, x1, gate2, final_g.reshape(1, d))


def _slots(tabs, rows, n_groups, n_experts, tile_e):
    n_tok_tiles = tabs.shape[0]
    first = tabs[:, n_groups:n_groups + n_experts, 0]
    blocks = tabs[:, n_groups:n_groups + n_experts, 1]
    run = blocks * SUBLANES
    total = jnp.sum(run, axis=0)
    region = ((total + tile_e - 1) // tile_e) * tile_e
    ends = jnp.cumsum(region)
    starts = ends - region
    gfirst = starts[None, :] + jnp.cumsum(run, axis=0) - run
    tab = jnp.concatenate([first, blocks, gfirst], axis=1).astype(I32)
    max_rows = 2 * rows + n_tok_tiles * n_experts * (SUBLANES - 1) + n_experts * (tile_e - SUBLANES)
    n_tiles = max_rows // tile_e
    n_valid = ends[-1] // tile_e
    tile_start = jnp.minimum(jnp.arange(n_tiles, dtype=I32), jnp.maximum(n_valid - 1, 0)) * tile_e
    tile_expert = jnp.sum((ends[None, :] <= tile_start[:, None]).astype(I32), axis=1)
    tile_expert = jnp.minimum(tile_expert, n_experts - 1).astype(I32)
    n_slots = n_tiles * tile_e
    zrow = jnp.concatenate([starts + total, ends[-1:]]).astype(I32)
    zcnt = (jnp.concatenate([region - total, n_slots - ends[-1:]]) // SUBLANES).astype(I32)
    return tab, tile_expert, n_valid.reshape(1).astype(I32), zrow, zcnt, n_slots


def kernel(x, c, w_ada, b_ada, norm1_g, w_in, conv_w, conv_b, conv_ln_g, conv_ln_b, ssm_a_re, ssm_a_im, ssm_b_re, ssm_b_im, ssm_c_re, ssm_c_im, ssm_d, ssm_log_dt, out_norm_conv_g, out_norm_ssm_g, w_out, norm2_g, router_group_w, router_group_b, router_expert_w, router_expert_b, exp_w_gate, exp_w_up, exp_w_down, final_norm_g):
    bsz, seq, d = x.shape
    assert bsz == SUBLANES
    depth = w_ada.shape[0]
    conv_ch = conv_w.shape[-1]
    ssm_ch = ssm_d.shape[-1]
    assert conv_ch == ssm_ch and w_in.shape[-1] == 2 * conv_ch + 2 * ssm_ch
    n_groups, per_group = router_expert_w.shape[1], router_expert_w.shape[3]
    n_experts = n_groups * per_group
    assert n_groups + n_experts <= ROUTE_LANES
    rows = bsz * seq
    tile_e = min(EXPERT_TILE, rows // 8)

    mod = _ada(c, w_ada, b_ada)
    abar_re, abar_im, bb_re, bb_im = _ssm_prep(ssm_a_re, ssm_a_im, ssm_log_dt, ssm_b_re, ssm_b_im)
    xt = x

    for l in range(depth):
        shift1, scale1, gate1, shift2, scale2, gate2 = [mod[l, :, k * d:(k + 1) * d] for k in range(6)]

        p = _inproj(xt, norm1_g[l], scale1, shift1, w_in[l].astype(BF16))
        if l == 0:
            p, xt = p
        yc = _conv(p, conv_w[l], conv_b[l], conv_ln_g[l], conv_ln_b[l], out_norm_conv_g[l])
        bc, cc, ar, ai = _ssm_block_weights(abar_re[l], abar_im[l], bb_re[l], bb_im[l], ssm_c_re[l], ssm_c_im[l])
        ys = _ssm(p, 2 * conv_ch // ssm_ch, bc, cc, ar, ai, ssm_d[l], out_norm_ssm_g[l])

        wr = jnp.concatenate([router_group_w[l]] + [router_expert_w[l, g] for g in range(n_groups)], axis=-1)
        wr = jnp.pad(wr, ((0, 0), (0, ROUTE_LANES - wr.shape[1]))).astype(BF16)
        br = jnp.concatenate([router_group_b[l], router_expert_b[l].reshape(-1)])
        br = jnp.pad(br, (0, ROUTE_LANES - br.shape[0])).reshape(1, ROUTE_LANES)
        wo = w_out[l].astype(BF16).reshape(2, conv_ch, d)
        x1, h2, lp, lpt, tabs = _outproj(yc, ys, wo, xt, gate1, norm2_g[l], scale2, shift2, wr, br,
                                         n_groups, per_group)

        tab, tile_expert, n_valid, zrow, zcnt, n_slots = _slots(tabs, rows, n_groups, n_experts, tile_e)
        hs = _dispatch(h2, lp, lpt, tab, zrow, zcnt, n_slots)
        eo = _experts(hs, exp_w_gate, exp_w_up, exp_w_down, l, tile_expert, n_valid, tile_e)
        xt = _combine(eo, tab, lp, x1, gate2, final_norm_g, n_experts, final=l == depth - 1)

    return xt
```

```python
import functools

import jax
import jax.numpy as jnp
from jax import lax
from jax.experimental import pallas as pl
from jax.experimental.pallas import tpu as pltpu

EPS = 1e-6
F32 = jnp.float32
BF16 = jnp.bfloat16
I32 = jnp.int32

SUBLANES = 8
LANES = 128
MXU_DIM = 256
VMEM_LIMIT = 56 * 1024 * 1024
ROUTE_LANES = LANES
ROW_CHUNK = 16
CONV_ROWS = 32
CONV_LANES = 512
EXPERT_TILE = 512
ZERO_ROWS = 256


def _params(sem):
    return pltpu.CompilerParams(dimension_semantics=sem, vmem_limit_bytes=VMEM_LIMIT)


def _for_rows(n_rows, chunk, body, unroll=2):
    def step(i, carry):
        body(pl.multiple_of(i * chunk, chunk))
        return carry
    lax.fori_loop(0, n_rows // chunk, step, 0, unroll=unroll)


def _per_batch(v, rows):
    return jnp.tile(v, (rows // SUBLANES, 1))


def _ada_kernel(c_ref, w_ref, b_ref, o_ref):
    c = c_ref[...]
    ca = (c * jax.nn.sigmoid(c)).astype(BF16)
    o_ref[...] = jnp.dot(ca, w_ref[...].astype(BF16), preferred_element_type=F32) + b_ref[...]


def _ada(c, w_ada, b_ada):
    depth, d, n = w_ada.shape
    bsz = c.shape[0]
    tn = min(512, n)
    return pl.pallas_call(
        _ada_kernel,
        grid=(depth, n // tn),
        in_specs=[pl.BlockSpec((bsz, d), lambda l, j: (0, 0)),
                  pl.BlockSpec((None, d, tn), lambda l, j: (l, 0, j)),
                  pl.BlockSpec((None, 1, tn), lambda l, j: (l, 0, j))],
        out_specs=pl.BlockSpec((None, bsz, tn), lambda l, j: (l, 0, j)),
        out_shape=jax.ShapeDtypeStruct((depth, bsz, n), F32),
        compiler_params=_params(("arbitrary", "arbitrary")),
    )(c, w_ada, b_ada.reshape(depth, 1, n))


def _inproj_kernel(x_ref, g_ref, sc_ref, sh_ref, w_ref, *rest, tn, batch_major):
    if batch_major:
        o_ref, xt_ref, h_scr = rest
        xt_ref[...] = jnp.swapaxes(x_ref[...], 0, 1).reshape(xt_ref.shape)
        x_ref = xt_ref
    else:
        o_ref, h_scr = rest
    tm = x_ref.shape[0]
    g = g_ref[...]
    sc = _per_batch(1.0 + sc_ref[...], ROW_CHUNK)
    sh = _per_batch(sh_ref[...], ROW_CHUNK)

    def chunk(r0):
        x = x_ref[pl.ds(r0, ROW_CHUNK), :]
        ms = jnp.mean(x * x, axis=-1, keepdims=True)
        h = x * lax.rsqrt(ms + EPS) * g
        h_scr[pl.ds(r0, ROW_CHUNK), :] = (h * sc + sh).astype(BF16)
    _for_rows(tm, ROW_CHUNK, chunk, unroll=4)

    for n0 in range(0, w_ref.shape[1], tn):
        o_ref[:, n0:n0 + tn] = jnp.dot(
            h_scr[...], w_ref[:, n0:n0 + tn], preferred_element_type=F32).astype(o_ref.dtype)


def _inproj(x, g, scale, shift, w_bf16):
    batch_major = x.ndim == 3
    d = x.shape[-1]
    rows = x.size // d
    n = w_bf16.shape[1]
    tm = min(512, rows)
    tn = min(1024, n)
    row_spec = lambda w: pl.BlockSpec((tm, w), lambda i: (i, 0))
    if batch_major:
        x_spec = pl.BlockSpec((SUBLANES, tm // SUBLANES, d), lambda i: (0, i, 0))
        out_specs = [row_spec(n), row_spec(d)]
        out_shape = [jax.ShapeDtypeStruct((rows, n), BF16), jax.ShapeDtypeStruct((rows, d), F32)]
    else:
        x_spec = row_spec(d)
        out_specs = row_spec(n)
        out_shape = jax.ShapeDtypeStruct((rows, n), BF16)
    return pl.pallas_call(
        functools.partial(_inproj_kernel, tn=tn, batch_major=batch_major),
        grid=(rows // tm,),
        in_specs=[x_spec,
                  pl.BlockSpec((1, d), lambda i: (0, 0)),
                  pl.BlockSpec((SUBLANES, d), lambda i: (0, 0)),
                  pl.BlockSpec((SUBLANES, d), lambda i: (0, 0)),
                  pl.BlockSpec((d, n), lambda i: (0, 0))],
        out_specs=out_specs,
        out_shape=out_shape,
        scratch_shapes=[pltpu.VMEM((tm, d), BF16)],
        compiler_params=_params(("arbitrary",)),
    )(x, g.reshape(1, d), scale, shift, w_bf16)


def _conv_kernel(v_ref, gt_ref, w_ref, cb_ref, lg_ref, lb_ref, og_ref, o_ref, z_scr, a_scr, *, width):
    rows, ch = v_ref.shape
    halo = (width - 1) * SUBLANES

    @pl.when(pl.program_id(0) == 0)
    def _():
        z_scr[pl.ds(0, halo), :] = jnp.zeros((halo, ch), F32)

    def glu(r0):
        v = v_ref[pl.ds(r0, ROW_CHUNK), :].astype(F32)
        gt = gt_ref[pl.ds(r0, ROW_CHUNK), :].astype(F32)
        z_scr[pl.ds(halo + r0, ROW_CHUNK), :] = v * jax.nn.sigmoid(gt)
    _for_rows(rows, ROW_CHUNK, glu)

    lg = lg_ref[...]
    lb = lb_ref[...]
    og = og_ref[...]
    reps = CONV_ROWS // SUBLANES
    lanes = min(CONV_LANES, ch)

    def taps(r0):
        for c0 in range(0, ch, lanes):
            cb = cb_ref[:, c0:c0 + lanes]
            acc = [cb] * reps
            w = {}
            for j in range(reps + width - 1):
                z = z_scr[pl.ds(r0 + j * SUBLANES, SUBLANES), c0:c0 + lanes]
                if j < width:
                    w[j] = w_ref[pl.ds(j * SUBLANES, SUBLANES), c0:c0 + lanes]
                for m in range(reps):
                    k = j - m
                    if 0 <= k < width:
                        acc[m] = acc[m] + w[k] * z
            for m in range(reps):
                a_scr[pl.ds(r0 + m * SUBLANES, SUBLANES), c0:c0 + lanes] = acc[m]
    _for_rows(rows, CONV_ROWS, taps, unroll=1)

    def conv(r0):
        acc = a_scr[pl.ds(r0, ROW_CHUNK), :]
        mu = jnp.mean(acc, axis=-1, keepdims=True)
        cen = acc - mu
        var = jnp.mean(cen * cen, axis=-1, keepdims=True)
        y = cen * lax.rsqrt(var + EPS) * lg + lb
        s = y * jax.nn.sigmoid(y)
        ms = jnp.mean(s * s, axis=-1, keepdims=True)
        o_ref[pl.ds(r0, ROW_CHUNK), :] = (s * lax.rsqrt(ms + EPS) * og).astype(BF16)
    _for_rows(rows, ROW_CHUNK, conv, unroll=8)

    z_scr[pl.ds(0, halo), :] = z_scr[pl.ds(rows, halo), :]


def _conv(p, conv_w, conv_b, ln_g, ln_b, out_g):
    rows = p.shape[0]
    width, ch = conv_w.shape
    halo = (width - 1) * SUBLANES
    tr = min(512, rows)
    assert tr >= halo
    vec = lambda a: a.reshape(1, ch)
    row_spec = lambda col: pl.BlockSpec((tr, ch), lambda i, col=col: (i, col))
    one = pl.BlockSpec((1, ch), lambda i: (0, 0))
    w8 = jnp.broadcast_to(conv_w[:, None, :], (width, SUBLANES, ch)).reshape(width * SUBLANES, ch)
    cb8 = jnp.broadcast_to(conv_b[None, :], (SUBLANES, ch))
    return pl.pallas_call(
        functools.partial(_conv_kernel, width=width),
        grid=(rows // tr,),
        in_specs=[row_spec(0), row_spec(1),
                  pl.BlockSpec((width * SUBLANES, ch), lambda i: (0, 0)),
                  pl.BlockSpec((SUBLANES, ch), lambda i: (0, 0)), one, one, one],
        out_specs=pl.BlockSpec((tr, ch), lambda i: (i, 0)),
        out_shape=jax.ShapeDtypeStruct((rows, ch), BF16),
        scratch_shapes=[pltpu.VMEM((halo + tr, ch), F32), pltpu.VMEM((tr, ch), F32)],
        compiler_params=_params(("arbitrary",)),
    )(p, p, w8, cb8, vec(ln_g), vec(ln_b), vec(out_g))


def _ssm_prep_kernel(lr_ref, li_ref, ldt_ref, br_ref, bi_ref, ar_o, ai_o, bbr_o, bbi_o):
    lr = lr_ref[...]
    li = li_ref[...]
    dt = jnp.exp(ldt_ref[...])
    mag = jnp.exp(lr * dt)
    ang = li * dt
    ar = mag * jnp.cos(ang)
    ai = mag * jnp.sin(ang)
    den = lr * lr + li * li
    nr = ar - 1.0
    ni = ai
    fr = (nr * lr + ni * li) / den
    fi = (ni * lr - nr * li) / den
    br = br_ref[...]
    bi = bi_ref[...]
    ar_o[...] = ar
    ai_o[...] = ai
    bbr_o[...] = fr * br - fi * bi
    bbi_o[...] = fr * bi + fi * br


def _ssm_prep(a_re, a_im, log_dt, b_re, b_im):
    depth, g, p = a_re.shape
    h = b_re.shape[-1]
    rep = lambda a: jnp.broadcast_to(a[:, :, None, :], (depth, g, h, p)).reshape(depth * g * h, p)
    ldt = jnp.broadcast_to(log_dt[:, :, None, None], (depth, g, h, p)).reshape(depth * g * h, p)
    bt = lambda b: jnp.swapaxes(b, -1, -2).reshape(depth * g * h, p)
    n = g * h
    spec = pl.BlockSpec((n, p), lambda l: (l, 0))
    shp = jax.ShapeDtypeStruct((depth * n, p), F32)
    ar, ai, bbr, bbi = pl.pallas_call(
        _ssm_prep_kernel,
        grid=(depth,),
        in_specs=[spec] * 5,
        out_specs=[spec] * 4,
        out_shape=[shp] * 4,
        compiler_params=_params(("arbitrary",)),
    )(rep(a_re), rep(a_im), ldt, bt(b_re), bt(b_im))
    r4 = lambda a: a.reshape(depth, g, h, p)
    return r4(ar)[:, :, 0, :], r4(ai)[:, :, 0, :], r4(bbr), r4(bbi)


def _ssm_block_weights(abar_re, abar_im, bb_re, bb_im, c_re, c_im):
    g, h, p = bb_re.shape
    gpt = min(MXU_DIM // h, g)
    kt = g // gpt
    eye = jnp.eye(gpt, dtype=F32)[None, :, None, :, None]

    def bmat(bb):
        b5 = bb.reshape(kt, gpt, h, 1, p)
        return (b5 * eye).astype(BF16).reshape(kt, gpt * h, gpt * p)

    def cmat(c):
        c5 = jnp.swapaxes(c.reshape(kt, gpt, h, p), 2, 3).reshape(kt, gpt, p, 1, h)
        return (c5 * eye).astype(BF16).reshape(kt, gpt * p, gpt * h)

    bc = jnp.concatenate([bmat(bb_re), bmat(bb_im)], axis=-1)
    cc = jnp.concatenate([cmat(c_re), cmat(-c_im)], axis=1)
    bcast = lambda a: jnp.broadcast_to(a.reshape(1, g * p), (SUBLANES, g * p))
    return bc, cc, bcast(abar_re), bcast(abar_im)


def _ssm_kernel(u_ref, gt_ref, bc_ref, cc_ref, ar_ref, ai_ref, d_ref, og_ref, o_ref,
                s_scr, st_scr, y_scr, *, lane_chunk):
    rows, ch = u_ref.shape
    kt, kch, sw2 = bc_ref.shape
    sw = sw2 // 2
    steps = rows // SUBLANES

    @pl.when(pl.program_id(0) == 0)
    def _():
        st_scr[...] = jnp.zeros(st_scr.shape, F32)

    for k in range(kt):
        ub = u_ref[:, k * kch:(k + 1) * kch].astype(BF16)
        s_scr[:, k * sw2:(k + 1) * sw2] = jnp.dot(ub, bc_ref[k], preferred_element_type=F32)

    for k in range(kt):
        for c in range(sw // lane_chunk):
            re0 = k * sw2 + c * lane_chunk
            im0 = re0 + sw
            a0 = k * sw + c * lane_chunk
            ar = ar_ref[:, a0:a0 + lane_chunk]
            ai = ai_ref[:, a0:a0 + lane_chunk]

            def step(t, carry, re0=re0, im0=im0, ar=ar, ai=ai):
                sr, si = carry
                r = pl.multiple_of(t * SUBLANES, SUBLANES)
                br = s_scr[pl.ds(r, SUBLANES), re0:re0 + lane_chunk]
                bi = s_scr[pl.ds(r, SUBLANES), im0:im0 + lane_chunk]
                nr = ar * sr - ai * si + br
                ni = ar * si + ai * sr + bi
                s_scr[pl.ds(r, SUBLANES), re0:re0 + lane_chunk] = nr
                s_scr[pl.ds(r, SUBLANES), im0:im0 + lane_chunk] = ni
                return nr, ni

            sr, si = lax.fori_loop(
                0, steps, step,
                (st_scr[:, re0:re0 + lane_chunk], st_scr[:, im0:im0 + lane_chunk]))
            st_scr[:, re0:re0 + lane_chunk] = sr
            st_scr[:, im0:im0 + lane_chunk] = si

    for k in range(kt):
        sb = s_scr[:, k * sw2:(k + 1) * sw2].astype(BF16)
        y_scr[:, k * kch:(k + 1) * kch] = jnp.dot(sb, cc_ref[k], preferred_element_type=F32)

    d = d_ref[...]
    og = og_ref[...]

    def gate(r0):
        u = u_ref[pl.ds(r0, ROW_CHUNK), :].astype(F32)
        y = y_scr[pl.ds(r0, ROW_CHUNK), :] + u * d
        v = jax.nn.gelu(y) * jax.nn.sigmoid(gt_ref[pl.ds(r0, ROW_CHUNK), :].astype(F32))
        ms = jnp.mean(v * v, axis=-1, keepdims=True)
        o_ref[pl.ds(r0, ROW_CHUNK), :] = (v * lax.rsqrt(ms + EPS) * og).astype(BF16)
    _for_rows(rows, ROW_CHUNK, gate, unroll=4)


def _ssm(p, col0, bc, cc, ar, ai, d_skip, out_g):
    rows = p.shape[0]
    kt, kch, sw2 = bc.shape
    ch = kt * kch
    tr = min(512, rows)
    lane_chunk = min(512, sw2 // 2)
    vec = lambda a: a.reshape(1, ch)
    row_spec = lambda col: pl.BlockSpec((tr, ch), lambda i, col=col: (i, col))
    one = pl.BlockSpec((1, ch), lambda i: (0, 0))
    full = lambda a: pl.BlockSpec(a.shape, lambda i, nd=a.ndim: (0,) * nd)
    return pl.pallas_call(
        functools.partial(_ssm_kernel, lane_chunk=lane_chunk),
        grid=(rows // tr,),
        in_specs=[row_spec(col0), row_spec(col0 + 1), full(bc), full(cc), full(ar), full(ai), one, one],
        out_specs=pl.BlockSpec((tr, ch), lambda i: (i, 0)),
        out_shape=jax.ShapeDtypeStruct((rows, ch), BF16),
        scratch_shapes=[pltpu.VMEM((tr, kt * sw2), F32),
                        pltpu.VMEM((SUBLANES, kt * sw2), F32),
                        pltpu.VMEM((tr, ch), F32)],
        compiler_params=_params(("arbitrary",)),
    )(p, p, bc, cc, ar, ai, vec(d_skip), vec(out_g))


def _outproj_kernel(yc_ref, ys_ref, wo_ref, x_ref, g1_ref, n2_ref, sc_ref, sh_ref, wr_ref, br_ref,
                    x1_ref, h2_ref, lp_ref, lpt_ref, tab_ref, *, n_groups, per_group):
    tm = x_ref.shape[0]

    x1_ref[...] = (jnp.dot(yc_ref[...], wo_ref[0], preferred_element_type=F32)
                   + jnp.dot(ys_ref[...], wo_ref[1], preferred_element_type=F32))

    g1 = _per_batch(g1_ref[...], ROW_CHUNK)
    sc = _per_batch(1.0 + sc_ref[...], ROW_CHUNK)
    sh = _per_batch(sh_ref[...], ROW_CHUNK)
    n2 = n2_ref[...]

    def chunk(r0):
        x1 = x_ref[pl.ds(r0, ROW_CHUNK), :] + g1 * x1_ref[pl.ds(r0, ROW_CHUNK), :]
        x1_ref[pl.ds(r0, ROW_CHUNK), :] = x1
        ms = jnp.mean(x1 * x1, axis=-1, keepdims=True)
        h2 = x1 * lax.rsqrt(ms + EPS) * n2 * sc + sh
        h2_ref[pl.ds(r0, ROW_CHUNK), :] = h2.astype(BF16)
    _for_rows(tm, ROW_CHUNK, chunk, unroll=4)

    lg = jnp.dot(h2_ref[...], wr_ref[...], preferred_element_type=F32) + br_ref[...]
    nl = tab_ref.shape[1]
    lt = jnp.transpose(lg)[0:nl, :]
    sub = lax.broadcasted_iota(I32, (nl, tm), 0)
    neg = jnp.float32(-jnp.inf)
    big = jnp.int32(ROUTE_LANES)

    def first_max(vals):
        m = jnp.max(vals, axis=0, keepdims=True)
        idx = jnp.min(jnp.where(vals == m, sub, big), axis=0, keepdims=True)
        return m, idx

    gl = jnp.where(sub < n_groups, lt, neg)
    gmax, gidx = first_max(gl)
    g_w = 1.0 / jnp.sum(jnp.exp(gl - gmax), axis=0, keepdims=True)
    lo = n_groups + gidx * per_group
    el = jnp.where((sub >= lo) & (sub < lo + per_group), lt, neg)
    v1, i1 = first_max(el)
    v2, i2 = first_max(jnp.where(sub == i1, neg, el))
    e21 = jnp.exp(v2 - v1)
    w1 = g_w / (1.0 + e21)
    w2 = g_w * e21 / (1.0 + e21)

    hit1 = sub == i1
    hit2 = sub == i2
    onehot = jnp.where(hit1 | hit2, 1.0, 0.0)
    row = lax.broadcasted_iota(I32, (tm, tm), 0)
    col = lax.broadcasted_iota(I32, (tm, tm), 1)
    before = jnp.where(row < col, 1.0, 0.0).astype(BF16)
    earlier = jnp.dot(onehot.astype(BF16), before, preferred_element_type=F32)
    count = jnp.sum(onehot, axis=1, keepdims=True).astype(I32)
    blocks = (count + (SUBLANES - 1)) // SUBLANES
    lrow = lax.broadcasted_iota(I32, (ROUTE_LANES, ROUTE_LANES), 0)
    lcol = lax.broadcasted_iota(I32, (ROUTE_LANES, ROUTE_LANES), 1)
    above = jnp.where(lcol < lrow, 1.0, 0.0).astype(BF16)
    blocks_sq = jnp.concatenate(
        [jnp.broadcast_to(blocks.astype(F32), (nl, ROUTE_LANES)),
         jnp.zeros((ROUTE_LANES - nl, ROUTE_LANES), F32)], axis=0).astype(BF16)
    first = jnp.dot(above, blocks_sq, preferred_element_type=F32)[0:nl, 0:1] * SUBLANES
    slot = first + earlier
    lpos1 = jnp.sum(jnp.where(hit1, slot, 0.0), axis=0, keepdims=True)
    lpos2 = jnp.sum(jnp.where(hit2, slot, 0.0), axis=0, keepdims=True)

    sub8 = lax.broadcasted_iota(I32, (SUBLANES, tm), 0)
    rec = jnp.where(sub8 == 0, lpos1, jnp.where(sub8 == 1, lpos2, jnp.where(
        sub8 == 2, w1, jnp.where(sub8 == 3, w2, 0.0))))
    lpt_ref[0] = rec
    lp_ref[...] = jnp.transpose(
        jnp.concatenate([rec, jnp.zeros((ROUTE_LANES - SUBLANES, tm), F32)], axis=0))
    lane = lax.broadcasted_iota(I32, (nl, ROUTE_LANES), 1)
    tab_ref[0] = jnp.where(lane == 0, first.astype(I32), jnp.where(lane == 1, blocks, 0))


def _outproj(yc, ys, wo_bf16, xt, gate1, n2_g, scale2, shift2, wr_bf16, br, n_groups, per_group):
    rows, d = xt.shape
    ch = yc.shape[1]
    tm = min(256, rows)
    row_spec = lambda w: pl.BlockSpec((tm, w), lambda i: (i, 0))
    one = pl.BlockSpec((1, d), lambda i: (0, 0))
    per_b = pl.BlockSpec((SUBLANES, d), lambda i: (0, 0))
    kern = functools.partial(_outproj_kernel, n_groups=n_groups, per_group=per_group)
    nl = n_groups + n_groups * per_group
    nl += -nl % SUBLANES
    return pl.pallas_call(
        kern,
        grid=(rows // tm,),
        in_specs=[row_spec(ch), row_spec(ch),
                  pl.BlockSpec((2, ch, d), lambda i: (0, 0, 0)),
                  row_spec(d), per_b, one, per_b, per_b,
                  pl.BlockSpec((d, ROUTE_LANES), lambda i: (0, 0)),
                  pl.BlockSpec((1, ROUTE_LANES), lambda i: (0, 0))],
        out_specs=[row_spec(d), row_spec(d), row_spec(ROUTE_LANES),
                   pl.BlockSpec((1, SUBLANES, tm), lambda i: (i, 0, 0)),
                   pl.BlockSpec((1, nl, ROUTE_LANES), lambda i: (i, 0, 0))],
        out_shape=[jax.ShapeDtypeStruct((rows, d), F32),
                   jax.ShapeDtypeStruct((rows, d), BF16),
                   jax.ShapeDtypeStruct((rows, ROUTE_LANES), F32),
                   jax.ShapeDtypeStruct((rows // tm, SUBLANES, tm), F32),
                   jax.ShapeDtypeStruct((rows // tm, nl, ROUTE_LANES), I32)],
        compiler_params=_params(("arbitrary",)),
    )(yc, ys, wo_bf16, xt, gate1, n2_g.reshape(1, d), scale2, shift2, wr_bf16, br)


def _for_blocks(tab_ref, tile, n_experts, fn):
    for e in range(n_experts):
        lrow0 = tab_ref[tile, e]
        grow0 = tab_ref[tile, 2 * n_experts + e]

        def block(c, carry, lrow0=lrow0, grow0=grow0):
            off = c * SUBLANES
            fn(pl.multiple_of(lrow0 + off, SUBLANES), pl.multiple_of(grow0 + off, SUBLANES))
            return carry
        lax.fori_loop(0, tab_ref[tile, n_experts + e], block, 0)


def _dispatch_kernel(zrow_ref, zcnt_ref, tab_ref, lpt_ref, lp_ref, h_ref, hs_ref,
                     sorted_scr, zero_scr, sem, zsem, *, n_experts):
    tm, d = h_ref.shape
    n_sorted = sorted_scr.shape[1]
    i = pl.program_id(0)
    slot = i % 2

    def each_clear(fn):
        big = zero_scr.shape[0]
        per_big = big // SUBLANES
        for e in range(n_experts + 1):
            row0 = zrow_ref[e]
            n_big = zcnt_ref[e] // per_big
            n_small = zcnt_ref[e] - n_big * per_big
            rest0 = row0 + n_big * big

            def big_block(c, carry, row0=row0):
                fn(pltpu.make_async_copy(
                    zero_scr, hs_ref.at[pl.ds(pl.multiple_of(row0 + c * big, SUBLANES), big)], zsem))
                return carry
            lax.fori_loop(0, n_big, big_block, 0)

            def small_block(c, carry, rest0=rest0):
                fn(pltpu.make_async_copy(
                    zero_scr.at[pl.ds(0, SUBLANES)],
                    hs_ref.at[pl.ds(pl.multiple_of(rest0 + c * SUBLANES, SUBLANES), SUBLANES)], zsem))
                return carry
            lax.fori_loop(0, n_small, small_block, 0)

    @pl.when(i == 0)
    def _():
        zero_scr[...] = jnp.zeros(zero_scr.shape, F32)
        each_clear(lambda cp: cp.start())

    srow = lax.broadcasted_iota(I32, (n_sorted, tm), 0)
    p1 = jnp.where(srow == lpt_ref[0, 0:1, :].astype(I32), 1.0, 0.0)
    p2 = jnp.where(srow == lpt_ref[0, 1:2, :].astype(I32), 1.0, 0.0)
    sorted_scr[slot, :, 0:d] = jnp.dot((p1 + p2).astype(BF16), h_ref[...], preferred_element_type=F32)

    lane = lax.broadcasted_iota(I32, (tm, ROUTE_LANES), 1)

    def pieces(w):
        hi = w.astype(BF16).astype(F32)
        mid = (w - hi).astype(BF16).astype(F32)
        lo = w - hi - mid
        return jnp.where(lane == 0, hi, jnp.where(lane == 1, mid, jnp.where(lane == 2, lo, 0.0))).astype(BF16)

    lp = lp_ref[...]
    sorted_scr[slot, :, d:d + ROUTE_LANES] = (
        jnp.dot(p1.astype(BF16), pieces(lp[:, 2:3]), preferred_element_type=F32)
        + jnp.dot(p2.astype(BF16), pieces(lp[:, 3:4]), preferred_element_type=F32))

    def copy(buf):
        return lambda lrow, grow: pltpu.make_async_copy(
            sorted_scr.at[buf, pl.ds(lrow, SUBLANES)], hs_ref.at[pl.ds(grow, SUBLANES)], sem.at[buf])

    _for_blocks(tab_ref, i, n_experts, lambda lrow, grow: copy(slot)(lrow, grow).start())

    @pl.when(i > 0)
    def _():
        _for_blocks(tab_ref, i - 1, n_experts, lambda lrow, grow: copy(1 - slot)(lrow, grow).wait())

    @pl.when(i == pl.num_programs(0) - 1)
    def _():
        _for_blocks(tab_ref, i, n_experts, lambda lrow, grow: copy(slot)(lrow, grow).wait())
        each_clear(lambda cp: cp.wait())


def _dispatch(h2, lp, lpt, tab, zrow, zcnt, n_slots):
    rows, d = h2.shape
    tm = lpt.shape[-1]
    n_experts = zrow.shape[0] - 1
    n_sorted = 2 * tm + n_experts * (SUBLANES - 1)
    n_sorted += -n_sorted % SUBLANES
    width = d + ROUTE_LANES
    grid_spec = pltpu.PrefetchScalarGridSpec(
        num_scalar_prefetch=3,
        grid=(rows // tm,),
        in_specs=[pl.BlockSpec((1, SUBLANES, tm), lambda i, *_: (i, 0, 0)),
                  pl.BlockSpec((tm, ROUTE_LANES), lambda i, *_: (i, 0)),
                  pl.BlockSpec((tm, d), lambda i, *_: (i, 0))],
        out_specs=pl.BlockSpec(memory_space=pl.ANY),
        scratch_shapes=[pltpu.VMEM((2, n_sorted, width), F32), pltpu.VMEM((ZERO_ROWS, width), F32),
                        pltpu.SemaphoreType.DMA((2,)), pltpu.SemaphoreType.DMA(())],
    )
    return pl.pallas_call(
        functools.partial(_dispatch_kernel, n_experts=n_experts),
        grid_spec=grid_spec,
        out_shape=jax.ShapeDtypeStruct((n_slots, width), F32),
        compiler_params=_params(("arbitrary",)),
    )(zrow, zcnt, tab, lpt, lp, h2)


def _expert_kernel(te_ref, nv_ref, hs_ref, wg_ref, wu_ref, wd_ref, eo_ref,
                   wg_scr, wu_scr, wd_scr, out_scr, zero_scr, sem, zsem):
    i = pl.program_id(0)
    last = pl.num_programs(0) - 1
    nv = nv_ref[0]
    slot = i % 2
    tile_e = out_scr.shape[1]
    zrows = zero_scr.shape[0]

    def each_clear(fn):
        def block(c, carry):
            row0 = pl.multiple_of(nv * tile_e + c * zrows, zrows)
            fn(pltpu.make_async_copy(zero_scr, eo_ref.at[pl.ds(row0, zrows)], zsem))
            return carry
        lax.fori_loop(0, (pl.num_programs(0) - nv) * (tile_e // zrows), block, 0)

    @pl.when(i == 0)
    def _():
        zero_scr[...] = jnp.zeros(zero_scr.shape, F32)
        each_clear(lambda cp: cp.start())

    def put(buf, tile):
        return pltpu.make_async_copy(
            out_scr.at[buf], eo_ref.at[pl.ds(pl.multiple_of(tile * tile_e, tile_e), tile_e)], sem.at[buf])

    @pl.when(i < nv)
    def _():
        @pl.when((i == 0) | (te_ref[i] != te_ref[jnp.maximum(i - 1, 0)]))
        def _():
            wg_scr[...] = wg_ref[...].astype(BF16)
            wu_scr[...] = wu_ref[...].astype(BF16)
            wd_scr[...] = wd_ref[...].astype(BF16)

        d = wg_scr.shape[0]
        hb = hs_ref[:, 0:d].astype(BF16)
        w = hs_ref[:, d:d + 1] + hs_ref[:, d + 1:d + 2] + hs_ref[:, d + 2:d + 3]
        a = jnp.dot(hb, wg_scr[...], preferred_element_type=F32)
        u = jnp.dot(hb, wu_scr[...], preferred_element_type=F32)
        hm = (a * jax.nn.sigmoid(a) * u * w).astype(BF16)
        out_scr[slot] = jnp.dot(hm, wd_scr[...], preferred_element_type=F32)
        put(slot, i).start()

    @pl.when((i > 0) & (i - 1 < nv))
    def _():
        put(1 - slot, i - 1).wait()

    @pl.when(i == last)
    def _():
        @pl.when(i < nv)
        def _():
            put(slot, i).wait()
        each_clear(lambda cp: cp.wait())


def _experts(hs, w_gate, w_up, w_down, layer, tile_expert, n_valid, tile_e):
    n_slots, width = hs.shape
    d = width - ROUTE_LANES
    per_group = w_gate.shape[2]
    f = w_gate.shape[-1]
    n_tiles = n_slots // tile_e

    def row_map(i, te, nv):
        return (jnp.minimum(i, nv[0] - 1), 0)

    def w_map(i, te, nv):
        e = te[i]
        return (layer, e // per_group, e % per_group, 0, 0)

    grid_spec = pltpu.PrefetchScalarGridSpec(
        num_scalar_prefetch=2,
        grid=(n_tiles,),
        in_specs=[pl.BlockSpec((tile_e, width), row_map),
                  pl.BlockSpec((None, None, None, d, f), w_map),
                  pl.BlockSpec((None, None, None, d, f), w_map),
                  pl.BlockSpec((None, None, None, f, d), w_map)],
        out_specs=pl.BlockSpec(memory_space=pl.ANY),
        scratch_shapes=[pltpu.VMEM((d, f), BF16), pltpu.VMEM((d, f), BF16), pltpu.VMEM((f, d), BF16),
                        pltpu.VMEM((2, tile_e, d), F32), pltpu.VMEM((min(ZERO_ROWS, tile_e), d), F32),
                        pltpu.SemaphoreType.DMA((2,)), pltpu.SemaphoreType.DMA(())],
    )
    return pl.pallas_call(
        _expert_kernel,
        grid_spec=grid_spec,
        out_shape=jax.ShapeDtypeStruct((n_slots, d), F32),
        compiler_params=_params(("arbitrary",)),
    )(tile_expert, n_valid, hs, w_gate, w_up, w_down)


def _combine_kernel(tab_ref, eo_ref, lp_ref, x_ref, g2_ref, gf_ref, o_ref, sorted_scr, y_scr, sem,
                    *rest, n_experts, final):
    tm, d = x_ref.shape
    n_sorted = sorted_scr.shape[1]
    i = pl.program_id(0)
    slot = i % 2

    def fetch(buf):
        return lambda lrow, grow: pltpu.make_async_copy(
            eo_ref.at[pl.ds(grow, SUBLANES)], sorted_scr.at[buf, pl.ds(lrow, SUBLANES)], sem.at[buf])

    @pl.when(i == 0)
    def _():
        for buf in range(2):
            sorted_scr[buf, pl.ds(2 * tm, n_sorted - 2 * tm), :] = jnp.zeros((n_sorted - 2 * tm, d), F32)
        _for_blocks(tab_ref, 0, n_experts, lambda lrow, grow: fetch(0)(lrow, grow).start())

    @pl.when(i + 1 < pl.num_programs(0))
    def _():
        _for_blocks(tab_ref, i + 1, n_experts, lambda lrow, grow: fetch(1 - slot)(lrow, grow).start())

    _for_blocks(tab_ref, i, n_experts, lambda lrow, grow: fetch(slot)(lrow, grow).wait())

    lp = lp_ref[...]
    scol = lax.broadcasted_iota(I32, (tm, n_sorted), 1)
    q = (jnp.where(scol == lp[:, 0:1].astype(I32), 1.0, 0.0)
         + jnp.where(scol == lp[:, 1:2].astype(I32), 1.0, 0.0)).astype(BF16)
    y_scr[...] = jnp.dot(q, sorted_scr[slot].astype(BF16), preferred_element_type=F32)

    g2 = _per_batch(g2_ref[...], ROW_CHUNK)
    gf = gf_ref[...]

    def chunk(r0):
        x2 = x_ref[pl.ds(r0, ROW_CHUNK), :] + g2 * y_scr[pl.ds(r0, ROW_CHUNK), :]
        if final:
            ms = jnp.mean(x2 * x2, axis=-1, keepdims=True)
            y_scr[pl.ds(r0, ROW_CHUNK), :] = x2 * lax.rsqrt(ms + EPS) * gf
        else:
            o_ref[pl.ds(r0, ROW_CHUNK), :] = x2
    _for_rows(tm, ROW_CHUNK, chunk, unroll=4)

    if final:
        osw_scr, osem = rest
        tl = tm // SUBLANES
        osw_scr[slot] = jnp.swapaxes(y_scr[...].reshape(tl, SUBLANES, d), 0, 1)

        def put(buf, tile, fn):
            for b in range(SUBLANES):
                fn(pltpu.make_async_copy(
                    osw_scr.at[buf, b], o_ref.at[b, pl.ds(pl.multiple_of(tile * tl, SUBLANES), tl)],
                    osem.at[buf]))

        put(slot, i, lambda cp: cp.start())

        @pl.when(i > 0)
        def _():
            put(1 - slot, i - 1, lambda cp: cp.wait())

        @pl.when(i == pl.num_programs(0) - 1)
        def _():
            put(slot, i, lambda cp: cp.wait())


def _combine(eo, tab, lp, x1, gate2, final_g, n_experts, final):
    rows, d = x1.shape
    tm = rows // tab.shape[0]
    n_sorted = 2 * tm + n_experts * (SUBLANES - 1)
    n_sorted += -n_sorted % SUBLANES
    scratch = [pltpu.VMEM((2, n_sorted, d), F32), pltpu.VMEM((tm, d), F32), pltpu.SemaphoreType.DMA((2,))]
    if final:
        out_spec = pl.BlockSpec(memory_space=pl.ANY)
        out_shape = jax.ShapeDtypeStruct((SUBLANES, rows // SUBLANES, d), F32)
        scratch += [pltpu.VMEM((2, SUBLANES, tm // SUBLANES, d), F32), pltpu.SemaphoreType.DMA((2,))]
    else:
        out_spec = pl.BlockSpec((tm, d), lambda i, *_: (i, 0))
        out_shape = jax.ShapeDtypeStruct((rows, d), F32)
    grid_spec = pltpu.PrefetchScalarGridSpec(
        num_scalar_prefetch=1,
        grid=(rows // tm,),
        in_specs=[pl.BlockSpec(memory_space=pl.ANY),
                  pl.BlockSpec((tm, ROUTE_LANES), lambda i, *_: (i, 0)),
                  pl.BlockSpec((tm, d), lambda i, *_: (i, 0)),
                  pl.BlockSpec((SUBLANES, d), lambda i, *_: (0, 0)),
                  pl.BlockSpec((1, d), lambda i, *_: (0, 0))],
        out_specs=out_spec,
        scratch_shapes=scratch,
    )
    return pl.pallas_call(
        functools.partial(_combine_kernel, n_experts=n_experts, final=final),
        grid_spec=grid_spec,
        out_shape=out_shape,
        compiler_params=_params(("arbitrary",)),
    )(tab, eo, lp, x1, gate2, final_g.reshape(1, d))


def _slots(tabs, rows, n_groups, n_experts, tile_e):
    n_tok_tiles = tabs.shape[0]
    first = tabs[:, n_groups:n_groups + n_experts, 0]
    blocks = tabs[:, n_groups:n_groups + n_experts, 1]
    run = blocks * SUBLANES
    total = jnp.sum(run, axis=0)
    region = ((total + tile_e - 1) // tile_e) * tile_e
    ends = jnp.cumsum(region)
    starts = ends - region
    gfirst = starts[None, :] + jnp.cumsum(run, axis=0) - run
    tab = jnp.concatenate([first, blocks, gfirst], axis=1).astype(I32)
    max_rows = 2 * rows + n_tok_tiles * n_experts * (SUBLANES - 1) + n_experts * (tile_e - SUBLANES)
    n_tiles = max_rows // tile_e
    n_valid = ends[-1] // tile_e
    tile_start = jnp.minimum(jnp.arange(n_tiles, dtype=I32), jnp.maximum(n_valid - 1, 0)) * tile_e
    tile_expert = jnp.sum((ends[None, :] <= tile_start[:, None]).astype(I32), axis=1)
    tile_expert = jnp.minimum(tile_expert, n_experts - 1).astype(I32)
    n_slots = n_tiles * tile_e
    zrow = jnp.concatenate([starts + total, ends[-1:]]).astype(I32)
    zcnt = (jnp.concatenate([region - total, n_slots - ends[-1:]]) // SUBLANES).astype(I32)
    return tab, tile_expert, n_valid.reshape(1).astype(I32), zrow, zcnt, n_slots


def kernel(x, c, w_ada, b_ada, norm1_g, w_in, conv_w, conv_b, conv_ln_g, conv_ln_b, ssm_a_re, ssm_a_im, ssm_b_re, ssm_b_im, ssm_c_re, ssm_c_im, ssm_d, ssm_log_dt, out_norm_conv_g, out_norm_ssm_g, w_out, norm2_g, router_group_w, router_group_b, router_expert_w, router_expert_b, exp_w_gate, exp_w_up, exp_w_down, final_norm_g):
    bsz, seq, d = x.shape
    assert bsz == SUBLANES
    depth = w_ada.shape[0]
    conv_ch = conv_w.shape[-1]
    ssm_ch = ssm_d.shape[-1]
    assert conv_ch == ssm_ch and w_in.shape[-1] == 2 * conv_ch + 2 * ssm_ch
    n_groups, per_group = router_expert_w.shape[1], router_expert_w.shape[3]
    n_experts = n_groups * per_group
    assert n_groups + n_experts <= ROUTE_LANES
    rows = bsz * seq
    tile_e = min(EXPERT_TILE, rows // 8)

    mod = _ada(c, w_ada, b_ada)
    abar_re, abar_im, bb_re, bb_im = _ssm_prep(ssm_a_re, ssm_a_im, ssm_log_dt, ssm_b_re, ssm_b_im)
    xt = x

    for l in range(depth):
        shift1, scale1, gate1, shift2, scale2, gate2 = [mod[l, :, k * d:(k + 1) * d] for k in range(6)]

        p = _inproj(xt, norm1_g[l], scale1, shift1, w_in[l].astype(BF16))
        if l == 0:
            p, xt = p
        yc = _conv(p, conv_w[l], conv_b[l], conv_ln_g[l], conv_ln_b[l], out_norm_conv_g[l])
        bc, cc, ar, ai = _ssm_block_weights(abar_re[l], abar_im[l], bb_re[l], bb_im[l], ssm_c_re[l], ssm_c_im[l])
        ys = _ssm(p, 2 * conv_ch // ssm_ch, bc, cc, ar, ai, ssm_d[l], out_norm_ssm_g[l])

        wr = jnp.concatenate([router_group_w[l]] + [router_expert_w[l, g] for g in range(n_groups)], axis=-1)
        wr = jnp.pad(wr, ((0, 0), (0, ROUTE_LANES - wr.shape[1]))).astype(BF16)
        br = jnp.concatenate([router_group_b[l], router_expert_b[l].reshape(-1)])
        br = jnp.pad(br, (0, ROUTE_LANES - br.shape[0])).reshape(1, ROUTE_LANES)
        wo = w_out[l].astype(BF16).reshape(2, conv_ch, d)
        x1, h2, lp, lpt, tabs = _outproj(yc, ys, wo, xt, gate1, norm2_g[l], scale2, shift2, wr, br,
                                         n_groups, per_group)

        tab, tile_expert, n_valid, zrow, zcnt, n_slots = _slots(tabs, rows, n_groups, n_experts, tile_e)
        hs = _dispatch(h2, lp, lpt, tab, zrow, zcnt, n_slots)
        eo = _experts(hs, exp_w_gate, exp_w_up, exp_w_down, l, tile_expert, n_valid, tile_e)
        xt = _combine(eo, tab, lp, x1, gate2, final_norm_g, n_experts, final=l == depth - 1)

    return xt
```

```python
import functools

import jax
import jax.numpy as jnp
from jax import lax
from jax.experimental import pallas as pl
from jax.experimental.pallas import tpu as pltpu

EPS = 1e-6
F32 = jnp.float32
BF16 = jnp.bfloat16
I32 = jnp.int32

SUBLANES = 8
LANES = 128
MXU_DIM = 256
VMEM_LIMIT = 56 * 1024 * 1024
ROUTE_LANES = LANES
ROW_CHUNK = 16
CONV_ROWS = 32
CONV_LANES = 512
EXPERT_TILE = 512
ZERO_ROWS = 256


def _params(sem):
    return pltpu.CompilerParams(dimension_semantics=sem, vmem_limit_bytes=VMEM_LIMIT)


def _for_rows(n_rows, chunk, body, unroll=2):
    def step(i, carry):
        body(pl.multiple_of(i * chunk, chunk))
        return carry
    lax.fori_loop(0, n_rows // chunk, step, 0, unroll=unroll)


def _per_batch(v, rows):
    return jnp.tile(v, (rows // SUBLANES, 1))


def _ada_kernel(c_ref, w_ref, b_ref, o_ref):
    c = c_ref[...]
    ca = (c * jax.nn.sigmoid(c)).astype(BF16)
    o_ref[...] = jnp.dot(ca, w_ref[...].astype(BF16), preferred_element_type=F32) + b_ref[...]


def _ada(c, w_ada, b_ada):
    depth, d, n = w_ada.shape
    bsz = c.shape[0]
    tn = min(512, n)
    return pl.pallas_call(
        _ada_kernel,
        grid=(depth, n // tn),
        in_specs=[pl.BlockSpec((bsz, d), lambda l, j: (0, 0)),
                  pl.BlockSpec((None, d, tn), lambda l, j: (l, 0, j)),
                  pl.BlockSpec((None, 1, tn), lambda l, j: (l, 0, j))],
        out_specs=pl.BlockSpec((None, bsz, tn), lambda l, j: (l, 0, j)),
        out_shape=jax.ShapeDtypeStruct((depth, bsz, n), F32),
        compiler_params=_params(("arbitrary", "arbitrary")),
    )(c, w_ada, b_ada.reshape(depth, 1, n))


def _inproj_kernel(x_ref, g_ref, sc_ref, sh_ref, w_ref, *rest, tn, batch_major):
    if batch_major:
        o_ref, xt_ref, h_scr = rest
        xt_ref[...] = jnp.swapaxes(x_ref[...], 0, 1).reshape(xt_ref.shape)
        x_ref = xt_ref
    else:
        o_ref, h_scr = rest
    tm = x_ref.shape[0]
    g = g_ref[...]
    sc = _per_batch(1.0 + sc_ref[...], ROW_CHUNK)
    sh = _per_batch(sh_ref[...], ROW_CHUNK)

    def chunk(r0):
        x = x_ref[pl.ds(r0, ROW_CHUNK), :]
        ms = jnp.mean(x * x, axis=-1, keepdims=True)
        h = x * lax.rsqrt(ms + EPS) * g
        h_scr[pl.ds(r0, ROW_CHUNK), :] = (h * sc + sh).astype(BF16)
    _for_rows(tm, ROW_CHUNK, chunk, unroll=4)

    for n0 in range(0, w_ref.shape[1], tn):
        o_ref[:, n0:n0 + tn] = jnp.dot(
            h_scr[...], w_ref[:, n0:n0 + tn], preferred_element_type=F32).astype(o_ref.dtype)


def _inproj(x, g, scale, shift, w_bf16):
    batch_major = x.ndim == 3
    d = x.shape[-1]
    rows = x.size // d
    n = w_bf16.shape[1]
    tm = min(512, rows)
    tn = min(1024, n)
    row_spec = lambda w: pl.BlockSpec((tm, w), lambda i: (i, 0))
    if batch_major:
        x_spec = pl.BlockSpec((SUBLANES, tm // SUBLANES, d), lambda i: (0, i, 0))
        out_specs = [row_spec(n), row_spec(d)]
        out_shape = [jax.ShapeDtypeStruct((rows, n), BF16), jax.ShapeDtypeStruct((rows, d), F32)]
    else:
        x_spec = row_spec(d)
        out_specs = row_spec(n)
        out_shape = jax.ShapeDtypeStruct((rows, n), BF16)
    return pl.pallas_call(
        functools.partial(_inproj_kernel, tn=tn, batch_major=batch_major),
        grid=(rows // tm,),
        in_specs=[x_spec,
                  pl.BlockSpec((1, d), lambda i: (0, 0)),
                  pl.BlockSpec((SUBLANES, d), lambda i: (0, 0)),
                  pl.BlockSpec((SUBLANES, d), lambda i: (0, 0)),
                  pl.BlockSpec((d, n), lambda i: (0, 0))],
        out_specs=out_specs,
        out_shape=out_shape,
        scratch_shapes=[pltpu.VMEM((tm, d), BF16)],
        compiler_params=_params(("arbitrary",)),
    )(x, g.reshape(1, d), scale, shift, w_bf16)


def _conv_kernel(v_ref, gt_ref, w_ref, cb_ref, lg_ref, lb_ref, og_ref, o_ref, z_scr, a_scr, *, width):
    rows, ch = v_ref.shape
    halo = (width - 1) * SUBLANES

    @pl.when(pl.program_id(0) == 0)
    def _():
        z_scr[pl.ds(0, halo), :] = jnp.zeros((halo, ch), F32)

    def glu(r0):
        v = v_ref[pl.ds(r0, ROW_CHUNK), :].astype(F32)
        gt = gt_ref[pl.ds(r0, ROW_CHUNK), :].astype(F32)
        z_scr[pl.ds(halo + r0, ROW_CHUNK), :] = v * jax.nn.sigmoid(gt)
    _for_rows(rows, ROW_CHUNK, glu)

    lg = lg_ref[...]
    lb = lb_ref[...]
    og = og_ref[...]
    reps = CONV_ROWS // SUBLANES
    lanes = min(CONV_LANES, ch)

    def taps(r0):
        for c0 in range(0, ch, lanes):
            cb = cb_ref[:, c0:c0 + lanes]
            acc = [cb] * reps
            w = {}
            for j in range(reps + width - 1):
                z = z_scr[pl.ds(r0 + j * SUBLANES, SUBLANES), c0:c0 + lanes]
                if j < width:
                    w[j] = w_ref[pl.ds(j * SUBLANES, SUBLANES), c0:c0 + lanes]
                for m in range(reps):
                    k = j - m
                    if 0 <= k < width:
                        acc[m] = acc[m] + w[k] * z
            for m in range(reps):
                a_scr[pl.ds(r0 + m * SUBLANES, SUBLANES), c0:c0 + lanes] = acc[m]
    _for_rows(rows, CONV_ROWS, taps, unroll=1)

    def conv(r0):
        acc = a_scr[pl.ds(r0, ROW_CHUNK), :]
        mu = jnp.mean(acc, axis=-1, keepdims=True)
        cen = acc - mu
        var = jnp.mean(cen * cen, axis=-1, keepdims=True)
        y = cen * lax.rsqrt(var + EPS) * lg + lb
        s = y * jax.nn.sigmoid(y)
        ms = jnp.mean(s * s, axis=-1, keepdims=True)
        o_ref[pl.ds(r0, ROW_CHUNK), :] = (s * lax.rsqrt(ms + EPS) * og).astype(BF16)
    _for_rows(rows, ROW_CHUNK, conv, unroll=8)

    z_scr[pl.ds(0, halo), :] = z_scr[pl.ds(rows, halo), :]


def _conv(p, conv_w, conv_b, ln_g, ln_b, out_g):
    rows = p.shape[0]
    width, ch = conv_w.shape
    halo = (width - 1) * SUBLANES
    tr = min(512, rows)
    assert tr >= halo
    vec = lambda a: a.reshape(1, ch)
    row_spec = lambda col: pl.BlockSpec((tr, ch), lambda i, col=col: (i, col))
    one = pl.BlockSpec((1, ch), lambda i: (0, 0))
    w8 = jnp.broadcast_to(conv_w[:, None, :], (width, SUBLANES, ch)).reshape(width * SUBLANES, ch)
    cb8 = jnp.broadcast_to(conv_b[None, :], (SUBLANES, ch))
    return pl.pallas_call(
        functools.partial(_conv_kernel, width=width),
        grid=(rows // tr,),
        in_specs=[row_spec(0), row_spec(1),
                  pl.BlockSpec((width * SUBLANES, ch), lambda i: (0, 0)),
                  pl.BlockSpec((SUBLANES, ch), lambda i: (0, 0)), one, one, one],
        out_specs=pl.BlockSpec((tr, ch), lambda i: (i, 0)),
        out_shape=jax.ShapeDtypeStruct((rows, ch), BF16),
        scratch_shapes=[pltpu.VMEM((halo + tr, ch), F32), pltpu.VMEM((tr, ch), F32)],
        compiler_params=_params(("arbitrary",)),
    )(p, p, w8, cb8, vec(ln_g), vec(ln_b), vec(out_g))


def _ssm_prep_kernel(lr_ref, li_ref, ldt_ref, br_ref, bi_ref, ar_o, ai_o, bbr_o, bbi_o):
    lr = lr_ref[...]
    li = li_ref[...]
    dt = jnp.exp(ldt_ref[...])
    mag = jnp.exp(lr * dt)
    ang = li * dt
    ar = mag * jnp.cos(ang)
    ai = mag * jnp.sin(ang)
    den = lr * lr + li * li
    nr = ar - 1.0
    ni = ai
    fr = (nr * lr + ni * li) / den
    fi = (ni * lr - nr * li) / den
    br = br_ref[...]
    bi = bi_ref[...]
    ar_o[...] = ar
    ai_o[...] = ai
    bbr_o[...] = fr * br - fi * bi
    bbi_o[...] = fr * bi + fi * br


def _ssm_prep(a_re, a_im, log_dt, b_re, b_im):
    depth, g, p = a_re.shape
    h = b_re.shape[-1]
    rep = lambda a: jnp.broadcast_to(a[:, :, None, :], (depth, g, h, p)).reshape(depth * g * h, p)
    ldt = jnp.broadcast_to(log_dt[:, :, None, None], (depth, g, h, p)).reshape(depth * g * h, p)
    bt = lambda b: jnp.swapaxes(b, -1, -2).reshape(depth * g * h, p)
    n = g * h
    spec = pl.BlockSpec((n, p), lambda l: (l, 0))
    shp = jax.ShapeDtypeStruct((depth * n, p), F32)
    ar, ai, bbr, bbi = pl.pallas_call(
        _ssm_prep_kernel,
        grid=(depth,),
        in_specs=[spec] * 5,
        out_specs=[spec] * 4,
        out_shape=[shp] * 4,
        compiler_params=_params(("arbitrary",)),
    )(rep(a_re), rep(a_im), ldt, bt(b_re), bt(b_im))
    r4 = lambda a: a.reshape(depth, g, h, p)
    return r4(ar)[:, :, 0, :], r4(ai)[:, :, 0, :], r4(bbr), r4(bbi)


def _ssm_block_weights(abar_re, abar_im, bb_re, bb_im, c_re, c_im):
    g, h, p = bb_re.shape
    gpt = min(MXU_DIM // h, g)
    kt = g // gpt
    eye = jnp.eye(gpt, dtype=F32)[None, :, None, :, None]

    def bmat(bb):
        b5 = bb.reshape(kt, gpt, h, 1, p)
        return (b5 * eye).astype(BF16).reshape(kt, gpt * h, gpt * p)

    def cmat(c):
        c5 = jnp.swapaxes(c.reshape(kt, gpt, h, p), 2, 3).reshape(kt, gpt, p, 1, h)
        return (c5 * eye).astype(BF16).reshape(kt, gpt * p, gpt * h)

    bc = jnp.concatenate([bmat(bb_re), bmat(bb_im)], axis=-1)
    cc = jnp.concatenate([cmat(c_re), cmat(-c_im)], axis=1)
    bcast = lambda a: jnp.broadcast_to(a.reshape(1, g * p), (SUBLANES, g * p))
    return bc, cc, bcast(abar_re), bcast(abar_im)


def _ssm_kernel(u_ref, gt_ref, bc_ref, cc_ref, ar_ref, ai_ref, d_ref, og_ref, o_ref,
                s_scr, st_scr, y_scr, *, lane_chunk):
    rows, ch = u_ref.shape
    kt, kch, sw2 = bc_ref.shape
    sw = sw2 // 2
    steps = rows // SUBLANES

    @pl.when(pl.program_id(0) == 0)
    def _():
        st_scr[...] = jnp.zeros(st_scr.shape, F32)

    for k in range(kt):
        ub = u_ref[:, k * kch:(k + 1) * kch].astype(BF16)
        s_scr[:, k * sw2:(k + 1) * sw2] = jnp.dot(ub, bc_ref[k], preferred_element_type=F32)

    for k in range(kt):
        for c in range(sw // lane_chunk):
            re0 = k * sw2 + c * lane_chunk
            im0 = re0 + sw
            a0 = k * sw + c * lane_chunk
            ar = ar_ref[:, a0:a0 + lane_chunk]
            ai = ai_ref[:, a0:a0 + lane_chunk]

            def step(t, carry, re0=re0, im0=im0, ar=ar, ai=ai):
                sr, si = carry
                r = pl.multiple_of(t * SUBLANES, SUBLANES)
                br = s_scr[pl.ds(r, SUBLANES), re0:re0 + lane_chunk]
                bi = s_scr[pl.ds(r, SUBLANES), im0:im0 + lane_chunk]
                nr = ar * sr - ai * si + br
                ni = ar * si + ai * sr + bi
                s_scr[pl.ds(r, SUBLANES), re0:re0 + lane_chunk] = nr
                s_scr[pl.ds(r, SUBLANES), im0:im0 + lane_chunk] = ni
                return nr, ni

            sr, si = lax.fori_loop(
                0, steps, step,
                (st_scr[:, re0:re0 + lane_chunk], st_scr[:, im0:im0 + lane_chunk]))
            st_scr[:, re0:re0 + lane_chunk] = sr
            st_scr[:, im0:im0 + lane_chunk] = si

    for k in range(kt):
        sb = s_scr[:, k * sw2:(k + 1) * sw2].astype(BF16)
        y_scr[:, k * kch:(k + 1) * kch] = jnp.dot(sb, cc_ref[k], preferred_element_type=F32)

    d = d_ref[...]
    og = og_ref[...]

    def gate(r0):
        u = u_ref[pl.ds(r0, ROW_CHUNK), :].astype(F32)
        y = y_scr[pl.ds(r0, ROW_CHUNK), :] + u * d
        v = jax.nn.gelu(y) * jax.nn.sigmoid(gt_ref[pl.ds(r0, ROW_CHUNK), :].astype(F32))
        ms = jnp.mean(v * v, axis=-1, keepdims=True)
        o_ref[pl.ds(r0, ROW_CHUNK), :] = (v * lax.rsqrt(ms + EPS) * og).astype(BF16)
    _for_rows(rows, ROW_CHUNK, gate, unroll=4)


def _ssm(p, col0, bc, cc, ar, ai, d_skip, out_g):
    rows = p.shape[0]
    kt, kch, sw2 = bc.shape
    ch = kt * kch
    tr = min(512, rows)
    lane_chunk = min(512, sw2 // 2)
    vec = lambda a: a.reshape(1, ch)
    row_spec = lambda col: pl.BlockSpec((tr, ch), lambda i, col=col: (i, col))
    one = pl.BlockSpec((1, ch), lambda i: (0, 0))
    full = lambda a: pl.BlockSpec(a.shape, lambda i, nd=a.ndim: (0,) * nd)
    return pl.pallas_call(
        functools.partial(_ssm_kernel, lane_chunk=lane_chunk),
        grid=(rows // tr,),
        in_specs=[row_spec(col0), row_spec(col0 + 1), full(bc), full(cc), full(ar), full(ai), one, one],
        out_specs=pl.BlockSpec((tr, ch), lambda i: (i, 0)),
        out_shape=jax.ShapeDtypeStruct((rows, ch), BF16),
        scratch_shapes=[pltpu.VMEM((tr, kt * sw2), F32),
                        pltpu.VMEM((SUBLANES, kt * sw2), F32),
                        pltpu.VMEM((tr, ch), F32)],
        compiler_params=_params(("arbitrary",)),
    )(p, p, bc, cc, ar, ai, vec(d_skip), vec(out_g))


def _outproj_kernel(yc_ref, ys_ref, wo_ref, x_ref, g1_ref, n2_ref, sc_ref, sh_ref, wr_ref, br_ref,
                    x1_ref, h2_ref, lp_ref, lpt_ref, tab_ref, *, n_groups, per_group):
    tm = x_ref.shape[0]

    x1_ref[...] = (jnp.dot(yc_ref[...], wo_ref[0], preferred_element_type=F32)
                   + jnp.dot(ys_ref[...], wo_ref[1], preferred_element_type=F32))

    g1 = _per_batch(g1_ref[...], ROW_CHUNK)
    sc = _per_batch(1.0 + sc_ref[...], ROW_CHUNK)
    sh = _per_batch(sh_ref[...], ROW_CHUNK)
    n2 = n2_ref[...]

    def chunk(r0):
        x1 = x_ref[pl.ds(r0, ROW_CHUNK), :] + g1 * x1_ref[pl.ds(r0, ROW_CHUNK), :]
        x1_ref[pl.ds(r0, ROW_CHUNK), :] = x1
        ms = jnp.mean(x1 * x1, axis=-1, keepdims=True)
        h2 = x1 * lax.rsqrt(ms + EPS) * n2 * sc + sh
        h2_ref[pl.ds(r0, ROW_CHUNK), :] = h2.astype(BF16)
    _for_rows(tm, ROW_CHUNK, chunk, unroll=4)

    lg = jnp.dot(h2_ref[...], wr_ref[...], preferred_element_type=F32) + br_ref[...]
    nl = tab_ref.shape[1]
    lt = jnp.transpose(lg)[0:nl, :]
    sub = lax.broadcasted_iota(I32, (nl, tm), 0)
    neg = jnp.float32(-jnp.inf)
    big = jnp.int32(ROUTE_LANES)

    def first_max(vals):
        m = jnp.max(vals, axis=0, keepdims=True)
        idx = jnp.min(jnp.where(vals == m, sub, big), axis=0, keepdims=True)
        return m, idx

    gl = jnp.where(sub < n_groups, lt, neg)
    gmax, gidx = first_max(gl)
    g_w = 1.0 / jnp.sum(jnp.exp(gl - gmax), axis=0, keepdims=True)
    lo = n_groups + gidx * per_group
    el = jnp.where((sub >= lo) & (sub < lo + per_group), lt, neg)
    v1, i1 = first_max(el)
    v2, i2 = first_max(jnp.where(sub == i1, neg, el))
    e21 = jnp.exp(v2 - v1)
    w1 = g_w / (1.0 + e21)
    w2 = g_w * e21 / (1.0 + e21)

    hit1 = sub == i1
    hit2 = sub == i2
    onehot = jnp.where(hit1 | hit2, 1.0, 0.0)
    row = lax.broadcasted_iota(I32, (tm, tm), 0)
    col = lax.broadcasted_iota(I32, (tm, tm), 1)
    before = jnp.where(row < col, 1.0, 0.0).astype(BF16)
    earlier = jnp.dot(onehot.astype(BF16), before, preferred_element_type=F32)
    count = jnp.sum(onehot, axis=1, keepdims=True).astype(I32)
    blocks = (count + (SUBLANES - 1)) // SUBLANES
    lrow = lax.broadcasted_iota(I32, (ROUTE_LANES, ROUTE_LANES), 0)
    lcol = lax.broadcasted_iota(I32, (ROUTE_LANES, ROUTE_LANES), 1)
    above = jnp.where(lcol < lrow, 1.0, 0.0).astype(BF16)
    blocks_sq = jnp.concatenate(
        [jnp.broadcast_to(blocks.astype(F32), (nl, ROUTE_LANES)),
         jnp.zeros((ROUTE_LANES - nl, ROUTE_LANES), F32)], axis=0).astype(BF16)
    first = jnp.dot(above, blocks_sq, preferred_element_type=F32)[0:nl, 0:1] * SUBLANES
    slot = first + earlier
    lpos1 = jnp.sum(jnp.where(hit1, slot, 0.0), axis=0, keepdims=True)
    lpos2 = jnp.sum(jnp.where(hit2, slot, 0.0), axis=0, keepdims=True)

    sub8 = lax.broadcasted_iota(I32, (SUBLANES, tm), 0)
    rec = jnp.where(sub8 == 0, lpos1, jnp.where(sub8 == 1, lpos2, jnp.where(
        sub8 == 2, w1, jnp.where(sub8 == 3, w2, 0.0))))
    lpt_ref[0] = rec
    lp_ref[...] = jnp.transpose(
        jnp.concatenate([rec, jnp.zeros((ROUTE_LANES - SUBLANES, tm), F32)], axis=0))
    lane = lax.broadcasted_iota(I32, (nl, ROUTE_LANES), 1)
    tab_ref[0] = jnp.where(lane == 0, first.astype(I32), jnp.where(lane == 1, blocks, 0))


def _outproj(yc, ys, wo_bf16, xt, gate1, n2_g, scale2, shift2, wr_bf16, br, n_groups, per_group):
    rows, d = xt.shape
    ch = yc.shape[1]
    tm = min(256, rows)
    row_spec = lambda w: pl.BlockSpec((tm, w), lambda i: (i, 0))
    one = pl.BlockSpec((1, d), lambda i: (0, 0))
    per_b = pl.BlockSpec((SUBLANES, d), lambda i: (0, 0))
    kern = functools.partial(_outproj_kernel, n_groups=n_groups, per_group=per_group)
    nl = n_groups + n_groups * per_group
    nl += -nl % SUBLANES
    return pl.pallas_call(
        kern,
        grid=(rows // tm,),
        in_specs=[row_spec(ch), row_spec(ch),
                  pl.BlockSpec((2, ch, d), lambda i: (0, 0, 0)),
                  row_spec(d), per_b, one, per_b, per_b,
                  pl.BlockSpec((d, ROUTE_LANES), lambda i: (0, 0)),
                  pl.BlockSpec((1, ROUTE_LANES), lambda i: (0, 0))],
        out_specs=[row_spec(d), row_spec(d), row_spec(ROUTE_LANES),
                   pl.BlockSpec((1, SUBLANES, tm), lambda i: (i, 0, 0)),
                   pl.BlockSpec((1, nl, ROUTE_LANES), lambda i: (i, 0, 0))],
        out_shape=[jax.ShapeDtypeStruct((rows, d), F32),
                   jax.ShapeDtypeStruct((rows, d), BF16),
                   jax.ShapeDtypeStruct((rows, ROUTE_LANES), F32),
                   jax.ShapeDtypeStruct((rows // tm, SUBLANES, tm), F32),
                   jax.ShapeDtypeStruct((rows // tm, nl, ROUTE_LANES), I32)],
        compiler_params=_params(("arbitrary",)),
    )(yc, ys, wo_bf16, xt, gate1, n2_g.reshape(1, d), scale2, shift2, wr_bf16, br)


def _for_blocks(tab_ref, tile, n_experts, fn):
    for e in range(n_experts):
        lrow0 = tab_ref[tile, e]
        grow0 = tab_ref[tile, 2 * n_experts + e]

        def block(c, carry, lrow0=lrow0, grow0=grow0):
            off = c * SUBLANES
            fn(pl.multiple_of(lrow0 + off, SUBLANES), pl.multiple_of(grow0 + off, SUBLANES))
            return carry
        lax.fori_loop(0, tab_ref[tile, n_experts + e], block, 0)


def _dispatch_kernel(zrow_ref, zcnt_ref, tab_ref, lpt_ref, lp_ref, h_ref, hs_ref,
                     sorted_scr, zero_scr, sem, zsem, *, n_experts):
    tm, d = h_ref.shape
    n_sorted = sorted_scr.shape[1]
    i = pl.program_id(0)
    slot = i % 2

    def each_clear(fn):
        big = zero_scr.shape[0]
        per_big = big // SUBLANES
        for e in range(n_experts + 1):
            row0 = zrow_ref[e]
            n_big = zcnt_ref[e] // per_big
            n_small = zcnt_ref[e] - n_big * per_big
            rest0 = row0 + n_big * big

            def big_block(c, carry, row0=row0):
                fn(pltpu.make_async_copy(
                    zero_scr, hs_ref.at[pl.ds(pl.multiple_of(row0 + c * big, SUBLANES), big)], zsem))
                return carry
            lax.fori_loop(0, n_big, big_block, 0)

            def small_block(c, carry, rest0=rest0):
                fn(pltpu.make_async_copy(
                    zero_scr.at[pl.ds(0, SUBLANES)],
                    hs_ref.at[pl.ds(pl.multiple_of(rest0 + c * SUBLANES, SUBLANES), SUBLANES)], zsem))
                return carry
            lax.fori_loop(0, n_small, small_block, 0)

    @pl.when(i == 0)
    def _():
        zero_scr[...] = jnp.zeros(zero_scr.shape, F32)
        each_clear(lambda cp: cp.start())

    srow = lax.broadcasted_iota(I32, (n_sorted, tm), 0)
    p1 = jnp.where(srow == lpt_ref[0, 0:1, :].astype(I32), 1.0, 0.0)
    p2 = jnp.where(srow == lpt_ref[0, 1:2, :].astype(I32), 1.0, 0.0)
    sorted_scr[slot, :, 0:d] = jnp.dot((p1 + p2).astype(BF16), h_ref[...], preferred_element_type=F32)

    lane = lax.broadcasted_iota(I32, (tm, ROUTE_LANES), 1)

    def pieces(w):
        hi = w.astype(BF16).astype(F32)
        mid = (w - hi).astype(BF16).astype(F32)
        lo = w - hi - mid
        return jnp.where(lane == 0, hi, jnp.where(lane == 1, mid, jnp.where(lane == 2, lo, 0.0))).astype(BF16)

    lp = lp_ref[...]
    sorted_scr[slot, :, d:d + ROUTE_LANES] = (
        jnp.dot(p1.astype(BF16), pieces(lp[:, 2:3]), preferred_element_type=F32)
        + jnp.dot(p2.astype(BF16), pieces(lp[:, 3:4]), preferred_element_type=F32))

    def copy(buf):
        return lambda lrow, grow: pltpu.make_async_copy(
            sorted_scr.at[buf, pl.ds(lrow, SUBLANES)], hs_ref.at[pl.ds(grow, SUBLANES)], sem.at[buf])

    _for_blocks(tab_ref, i, n_experts, lambda lrow, grow: copy(slot)(lrow, grow).start())

    @pl.when(i > 0)
    def _():
        _for_blocks(tab_ref, i - 1, n_experts, lambda lrow, grow: copy(1 - slot)(lrow, grow).wait())

    @pl.when(i == pl.num_programs(0) - 1)
    def _():
        _for_blocks(tab_ref, i, n_experts, lambda lrow, grow: copy(slot)(lrow, grow).wait())
        each_clear(lambda cp: cp.wait())


def _dispatch(h2, lp, lpt, tab, zrow, zcnt, n_slots):
    rows, d = h2.shape
    tm = lpt.shape[-1]
    n_experts = zrow.shape[0] - 1
    n_sorted = 2 * tm + n_experts * (SUBLANES - 1)
    n_sorted += -n_sorted % SUBLANES
    width = d + ROUTE_LANES
    grid_spec = pltpu.PrefetchScalarGridSpec(
        num_scalar_prefetch=3,
        grid=(rows // tm,),
        in_specs=[pl.BlockSpec((1, SUBLANES, tm), lambda i, *_: (i, 0, 0)),
                  pl.BlockSpec((tm, ROUTE_LANES), lambda i, *_: (i, 0)),
                  pl.BlockSpec((tm, d), lambda i, *_: (i, 0))],
        out_specs=pl.BlockSpec(memory_space=pl.ANY),
        scratch_shapes=[pltpu.VMEM((2, n_sorted, width), F32), pltpu.VMEM((ZERO_ROWS, width), F32),
                        pltpu.SemaphoreType.DMA((2,)), pltpu.SemaphoreType.DMA(())],
    )
    return pl.pallas_call(
        functools.partial(_dispatch_kernel, n_experts=n_experts),
        grid_spec=grid_spec,
        out_shape=jax.ShapeDtypeStruct((n_slots, width), F32),
        compiler_params=_params(("arbitrary",)),
    )(zrow, zcnt, tab, lpt, lp, h2)


def _expert_kernel(te_ref, nv_ref, hs_ref, wg_ref, wu_ref, wd_ref, eo_ref,
                   wg_scr, wu_scr, wd_scr, out_scr, zero_scr, hs_scr, sem, zsem, hsem):
    i = pl.program_id(0)
    last = pl.num_programs(0) - 1
    nv = nv_ref[0]
    slot = i % 2
    tile_e = out_scr.shape[1]
    zrows = zero_scr.shape[0]

    def each_clear(fn):
        def block(c, carry):
            row0 = pl.multiple_of(nv * tile_e + c * zrows, zrows)
            fn(pltpu.make_async_copy(zero_scr, eo_ref.at[pl.ds(row0, zrows)], zsem))
            return carry
        lax.fori_loop(0, (pl.num_programs(0) - nv) * (tile_e // zrows), block, 0)

    @pl.when(i == 0)
    def _():
        zero_scr[...] = jnp.zeros(zero_scr.shape, F32)
        each_clear(lambda cp: cp.start())

    def put(buf, tile):
        return pltpu.make_async_copy(
            out_scr.at[buf], eo_ref.at[pl.ds(pl.multiple_of(tile * tile_e, tile_e), tile_e)], sem.at[buf])

    def get(tile):
        buf = tile % 3
        return pltpu.make_async_copy(
            hs_ref.at[pl.ds(pl.multiple_of(tile * tile_e, tile_e), tile_e)], hs_scr.at[buf], hsem.at[buf])

    @pl.when(i == 0)
    def _():
        for t in range(2):
            @pl.when(t < nv)
            def _(t=t):
                get(t).start()

    @pl.when(i < nv)
    def _():
        @pl.when(i + 2 < nv)
        def _():
            get(i + 2).start()

        @pl.when((i == 0) | (te_ref[i] != te_ref[jnp.maximum(i - 1, 0)]))
        def _():
            wg_scr[...] = wg_ref[...].astype(BF16)
            wu_scr[...] = wu_ref[...].astype(BF16)
            wd_scr[...] = wd_ref[...].astype(BF16)

        get(i).wait()
        d = wg_scr.shape[0]
        hs = hs_scr.at[i % 3]
        hb = hs[:, 0:d].astype(BF16)
        w = hs[:, d:d + 1] + hs[:, d + 1:d + 2] + hs[:, d + 2:d + 3]
        a = jnp.dot(hb, wg_scr[...], preferred_element_type=F32)
        u = jnp.dot(hb, wu_scr[...], preferred_element_type=F32)
        hm = (a * jax.nn.sigmoid(a) * u * w).astype(BF16)
        out_scr[slot] = jnp.dot(hm, wd_scr[...], preferred_element_type=F32)
        put(slot, i).start()

    @pl.when((i > 0) & (i - 1 < nv))
    def _():
        put(1 - slot, i - 1).wait()

    @pl.when(i == last)
    def _():
        @pl.when(i < nv)
        def _():
            put(slot, i).wait()
        each_clear(lambda cp: cp.wait())


def _experts(hs, w_gate, w_up, w_down, layer, tile_expert, n_valid, tile_e):
    n_slots, width = hs.shape
    d = width - ROUTE_LANES
    per_group = w_gate.shape[2]
    f = w_gate.shape[-1]
    n_tiles = n_slots // tile_e

    def row_map(i, te, nv):
        return (jnp.minimum(i, nv[0] - 1), 0)

    def w_map(i, te, nv):
        e = te[i]
        return (layer, e // per_group, e % per_group, 0, 0)

    grid_spec = pltpu.PrefetchScalarGridSpec(
        num_scalar_prefetch=2,
        grid=(n_tiles,),
        in_specs=[pl.BlockSpec(memory_space=pl.ANY),
                  pl.BlockSpec((None, None, None, d, f), w_map),
                  pl.BlockSpec((None, None, None, d, f), w_map),
                  pl.BlockSpec((None, None, None, f, d), w_map)],
        out_specs=pl.BlockSpec(memory_space=pl.ANY),
        scratch_shapes=[pltpu.VMEM((d, f), BF16), pltpu.VMEM((d, f), BF16), pltpu.VMEM((f, d), BF16),
                        pltpu.VMEM((2, tile_e, d), F32), pltpu.VMEM((min(ZERO_ROWS, tile_e), d), F32),
                        pltpu.VMEM((3, tile_e, width), F32),
                        pltpu.SemaphoreType.DMA((2,)), pltpu.SemaphoreType.DMA(()),
                        pltpu.SemaphoreType.DMA((3,))],
    )
    return pl.pallas_call(
        _expert_kernel,
        grid_spec=grid_spec,
        out_shape=jax.ShapeDtypeStruct((n_slots, d), F32),
        compiler_params=_params(("arbitrary",)),
    )(tile_expert, n_valid, hs, w_gate, w_up, w_down)


def _combine_kernel(tab_ref, eo_ref, lp_ref, x_ref, g2_ref, gf_ref, o_ref, sorted_scr, y_scr, sem,
                    *, n_experts, final):
    tm, d = x_ref.shape
    n_sorted = sorted_scr.shape[1]
    i = pl.program_id(0)
    slot = i % 2

    def fetch(buf):
        return lambda lrow, grow: pltpu.make_async_copy(
            eo_ref.at[pl.ds(grow, SUBLANES)], sorted_scr.at[buf, pl.ds(lrow, SUBLANES)], sem.at[buf])

    @pl.when(i == 0)
    def _():
        for buf in range(2):
            sorted_scr[buf, pl.ds(2 * tm, n_sorted - 2 * tm), :] = jnp.zeros((n_sorted - 2 * tm, d), F32)
        _for_blocks(tab_ref, 0, n_experts, lambda lrow, grow: fetch(0)(lrow, grow).start())

    @pl.when(i + 1 < pl.num_programs(0))
    def _():
        _for_blocks(tab_ref, i + 1, n_experts, lambda lrow, grow: fetch(1 - slot)(lrow, grow).start())

    _for_blocks(tab_ref, i, n_experts, lambda lrow, grow: fetch(slot)(lrow, grow).wait())

    lp = lp_ref[...]
    scol = lax.broadcasted_iota(I32, (tm, n_sorted), 1)
    q = (jnp.where(scol == lp[:, 0:1].astype(I32), 1.0, 0.0)
         + jnp.where(scol == lp[:, 1:2].astype(I32), 1.0, 0.0)).astype(BF16)
    y_scr[...] = jnp.dot(q, sorted_scr[slot].astype(BF16), preferred_element_type=F32)

    g2 = _per_batch(g2_ref[...], ROW_CHUNK)
    gf = gf_ref[...]

    def chunk(r0):
        x2 = x_ref[pl.ds(r0, ROW_CHUNK), :] + g2 * y_scr[pl.ds(r0, ROW_CHUNK), :]
        if final:
            ms = jnp.mean(x2 * x2, axis=-1, keepdims=True)
            y_scr[pl.ds(r0, ROW_CHUNK), :] = x2 * lax.rsqrt(ms + EPS) * gf
        else:
            o_ref[pl.ds(r0, ROW_CHUNK), :] = x2
    _for_rows(tm, ROW_CHUNK, chunk, unroll=4)

    if final:
        o_ref[...] = jnp.swapaxes(y_scr[...].reshape(tm // SUBLANES, SUBLANES, d), 0, 1)


def _combine(eo, tab, lp, x1, gate2, final_g, n_experts, final):
    rows, d = x1.shape
    tm = rows // tab.shape[0]
    n_sorted = 2 * tm + n_experts * (SUBLANES - 1)
    n_sorted += -n_sorted % SUBLANES
    if final:
        out_spec = pl.BlockSpec((SUBLANES, tm // SUBLANES, d), lambda i, *_: (0, i, 0))
        out_shape = jax.ShapeDtypeStruct((SUBLANES, rows // SUBLANES, d), F32)
    else:
        out_spec = pl.BlockSpec((tm, d), lambda i, *_: (i, 0))
        out_shape = jax.ShapeDtypeStruct((rows, d), F32)
    grid_spec = pltpu.PrefetchScalarGridSpec(
        num_scalar_prefetch=1,
        grid=(rows // tm,),
        in_specs=[pl.BlockSpec(memory_space=pl.ANY),
                  pl.BlockSpec((tm, ROUTE_LANES), lambda i, *_: (i, 0)),
                  pl.BlockSpec((tm, d), lambda i, *_: (i, 0)),
                  pl.BlockSpec((SUBLANES, d), lambda i, *_: (0, 0)),
                  pl.BlockSpec((1, d), lambda i, *_: (0, 0))],
        out_specs=out_spec,
        scratch_shapes=[pltpu.VMEM((2, n_sorted, d), F32), pltpu.VMEM((tm, d), F32),
                        pltpu.SemaphoreType.DMA((2,))],
    )
    return pl.pallas_call(
        functools.partial(_combine_kernel, n_experts=n_experts, final=final),
        grid_spec=grid_spec,
        out_shape=out_shape,
        compiler_params=_params(("arbitrary",)),
    )(tab, eo, lp, x1, gate2, final_g.reshape(1, d))


def _slots(tabs, rows, n_groups, n_experts, tile_e):
    n_tok_tiles = tabs.shape[0]
    first = tabs[:, n_groups:n_groups + n_experts, 0]
    blocks = tabs[:, n_groups:n_groups + n_experts, 1]
    run = blocks * SUBLANES
    total = jnp.sum(run, axis=0)
    region = ((total + tile_e - 1) // tile_e) * tile_e
    ends = jnp.cumsum(region)
    starts = ends - region
    gfirst = starts[None, :] + jnp.cumsum(run, axis=0) - run
    tab = jnp.concatenate([first, blocks, gfirst], axis=1).astype(I32)
    max_rows = 2 * rows + n_tok_tiles * n_experts * (SUBLANES - 1) + n_experts * (tile_e - SUBLANES)
    n_tiles = max_rows // tile_e
    n_valid = ends[-1] // tile_e
    tile_start = jnp.minimum(jnp.arange(n_tiles, dtype=I32), jnp.maximum(n_valid - 1, 0)) * tile_e
    tile_expert = jnp.sum((ends[None, :] <= tile_start[:, None]).astype(I32), axis=1)
    tile_expert = jnp.minimum(tile_expert, n_experts - 1).astype(I32)
    n_slots = n_tiles * tile_e
    zrow = jnp.concatenate([starts + total, ends[-1:]]).astype(I32)
    zcnt = (jnp.concatenate([region - total, n_slots - ends[-1:]]) // SUBLANES).astype(I32)
    return tab, tile_expert, n_valid.reshape(1).astype(I32), zrow, zcnt, n_slots


def kernel(x, c, w_ada, b_ada, norm1_g, w_in, conv_w, conv_b, conv_ln_g, conv_ln_b, ssm_a_re, ssm_a_im, ssm_b_re, ssm_b_im, ssm_c_re, ssm_c_im, ssm_d, ssm_log_dt, out_norm_conv_g, out_norm_ssm_g, w_out, norm2_g, router_group_w, router_group_b, router_expert_w, router_expert_b, exp_w_gate, exp_w_up, exp_w_down, final_norm_g):
    bsz, seq, d = x.shape
    assert bsz == SUBLANES
    depth = w_ada.shape[0]
    conv_ch = conv_w.shape[-1]
    ssm_ch = ssm_d.shape[-1]
    assert conv_ch == ssm_ch and w_in.shape[-1] == 2 * conv_ch + 2 * ssm_ch
    n_groups, per_group = router_expert_w.shape[1], router_expert_w.shape[3]
    n_experts = n_groups * per_group
    assert n_groups + n_experts <= ROUTE_LANES
    rows = bsz * seq
    tile_e = min(EXPERT_TILE, rows // 8)

    mod = _ada(c, w_ada, b_ada)
    abar_re, abar_im, bb_re, bb_im = _ssm_prep(ssm_a_re, ssm_a_im, ssm_log_dt, ssm_b_re, ssm_b_im)
    xt = x

    for l in range(depth):
        shift1, scale1, gate1, shift2, scale2, gate2 = [mod[l, :, k * d:(k + 1) * d] for k in range(6)]

        p = _inproj(xt, norm1_g[l], scale1, shift1, w_in[l].astype(BF16))
        if l == 0:
            p, xt = p
        yc = _conv(p, conv_w[l], conv_b[l], conv_ln_g[l], conv_ln_b[l], out_norm_conv_g[l])
        bc, cc, ar, ai = _ssm_block_weights(abar_re[l], abar_im[l], bb_re[l], bb_im[l], ssm_c_re[l], ssm_c_im[l])
        ys = _ssm(p, 2 * conv_ch // ssm_ch, bc, cc, ar, ai, ssm_d[l], out_norm_ssm_g[l])

        wr = jnp.concatenate([router_group_w[l]] + [router_expert_w[l, g] for g in range(n_groups)], axis=-1)
        wr = jnp.pad(wr, ((0, 0), (0, ROUTE_LANES - wr.shape[1]))).astype(BF16)
        br = jnp.concatenate([router_group_b[l], router_expert_b[l].reshape(-1)])
        br = jnp.pad(br, (0, ROUTE_LANES - br.shape[0])).reshape(1, ROUTE_LANES)
        wo = w_out[l].astype(BF16).reshape(2, conv_ch, d)
        x1, h2, lp, lpt, tabs = _outproj(yc, ys, wo, xt, gate1, norm2_g[l], scale2, shift2, wr, br,
                                         n_groups, per_group)

        tab, tile_expert, n_valid, zrow, zcnt, n_slots = _slots(tabs, rows, n_groups, n_experts, tile_e)
        hs = _dispatch(h2, lp, lpt, tab, zrow, zcnt, n_slots)
        eo = _experts(hs, exp_w_gate, exp_w_up, exp_w_down, l, tile_expert, n_valid, tile_e)
        xt = _combine(eo, tab, lp, x1, gate2, final_norm_g, n_experts, final=l == depth - 1)

    return xt
```
